```python
import jax, jax.numpy as jnp
from jax import lax
import numpy as np

D_MODEL = 1024
BATCH = 16
SEQ = 2048
DEPTH = 1
DEC_BATCH = 128
DEC_SEQ = 1
PAST_LEN = 16384
PAGE_SIZE = 128

CHUNK = 128
GM_HEADS = 8
GM_DIM = D_MODEL // 2 // GM_HEADS
GM_WIDTH = GM_HEADS * GM_DIM
N_HEADS = 8
KV_HEADS = 2
HEAD_DIM = 64
GROUP = N_HEADS // KV_HEADS
WINDOW = 128
ATT_BLOCK = 128
ROPE_THETA = 10000.0
Q_WIDTH = N_HEADS * HEAD_DIM
KV_WIDTH = KV_HEADS * HEAD_DIM
D_FF = 3 * D_MODEL
CONV_W = 3
EPS = 1e-6
IN_WIDTH = 2 * GM_WIDTH + Q_WIDTH + 2 * KV_WIDTH + 2 * D_MODEL

kernel_name = 'hybrid_gmlp_swa_sink_convffn_step'


def rms_norm(x, g):
    xf = x.astype(jnp.float32)
    y = xf * lax.rsqrt(jnp.mean(xf * xf, axis=-1, keepdims=True) + EPS)
    return (y * g.astype(jnp.float32)).astype(x.dtype)


def layer_norm(x, g, b):
    xf = x.astype(jnp.float32)
    mu = jnp.mean(xf, axis=-1, keepdims=True)
    var = jnp.mean(jnp.square(xf - mu), axis=-1, keepdims=True)
    y = (xf - mu) * lax.rsqrt(var + EPS)
    return (y * g.astype(jnp.float32) + b.astype(jnp.float32)).astype(x.dtype)


def rotary(x, pos):
    half = HEAD_DIM // 2
    inv = ROPE_THETA ** (-jnp.arange(half, dtype=jnp.float32) / half)
    ang = pos.astype(jnp.float32)[:, None] * inv[None, :]
    cos = jnp.cos(ang)[None, :, None, :]
    sin = jnp.sin(ang)[None, :, None, :]
    xf = x.astype(jnp.float32)
    x1, x2 = xf[..., :half], xf[..., half:]
    return jnp.concatenate([x1 * cos - x2 * sin, x2 * cos + x1 * sin], axis=-1).astype(x.dtype)


def sink_softmax(s, mask, sink):
    s = jnp.where(mask, s, -jnp.inf)
    sink = sink.astype(jnp.float32)
    m = jnp.maximum(jnp.max(s, axis=-1, keepdims=True), sink)
    p = jnp.exp(s - m)
    return p / (jnp.sum(p, axis=-1, keepdims=True) + jnp.exp(sink - m))


def in_projection(h, w_in, ln_v_g, ln_v_b):
    B, S = h.shape[0], h.shape[1]
    z = h @ w_in
    a = jax.nn.gelu(z[..., :2 * GM_WIDTH])
    u = a[..., :GM_WIDTH]
    vg = layer_norm(a[..., GM_WIDTH:], ln_v_g, ln_v_b).reshape(B, S, GM_HEADS, GM_DIM)
    o = 2 * GM_WIDTH
    q = z[..., o:o + Q_WIDTH].reshape(B, S, N_HEADS, HEAD_DIM)
    o += Q_WIDTH
    k = z[..., o:o + KV_WIDTH].reshape(B, S, KV_HEADS, HEAD_DIM)
    o += KV_WIDTH
    v = z[..., o:o + KV_WIDTH].reshape(B, S, KV_HEADS, HEAD_DIM)
    o += KV_WIDTH
    ga = z[..., o:o + D_MODEL]
    gb = z[..., o + D_MODEL:]
    return u, vg, q, k, v, ga, gb


def spatial_gate(u, vg, w_s, b_s):
    B, S = u.shape[0], u.shape[1]
    T = min(S, CHUNK)
    nc = S // T
    causal = jnp.tril(jnp.ones((T, T), dtype=bool))
    ws = jnp.where(causal[None], w_s[:, :T, :T], 0.0).astype(vg.dtype)
    vc = vg.reshape(B, nc, T, GM_HEADS, GM_DIM)
    mix = jnp.einsum('hts,bcshd->bcthd', ws, vc) + b_s[:, :T].T[:, :, None].astype(vg.dtype)
    return u * mix.reshape(B, S, GM_WIDTH)


def swa_prompt(q, k, v, sinks):
    B, S = q.shape[0], q.shape[1]
    T = ATT_BLOCK
    nb = S // T
    qb = q.reshape(B, nb, T, KV_HEADS, GROUP, HEAD_DIM)
    kb = k.reshape(B, nb, T, KV_HEADS, HEAD_DIM)
    vb = v.reshape(B, nb, T, KV_HEADS, HEAD_DIM)
    kprev = jnp.concatenate([jnp.zeros_like(kb[:, :1]), kb[:, :-1]], axis=1)
    vprev = jnp.concatenate([jnp.zeros_like(vb[:, :1]), vb[:, :-1]], axis=1)
    kk = jnp.concatenate([kprev, kb], axis=2)
    vv = jnp.concatenate([vprev, vb], axis=2)
    s = jnp.einsum('bcqhgd,bckhd->bchgqk', qb, kk).astype(jnp.float32) * (HEAD_DIM ** -0.5)
    i = jnp.arange(T)[:, None]
    j = jnp.arange(2 * T)[None, :]
    diff = i + T - j
    band = (diff >= 0) & (diff <= WINDOW)
    blk = jnp.arange(nb)[:, None, None]
    mask = band[None] & ((blk > 0) | (j >= T)[None])
    sink = sinks.reshape(KV_HEADS, GROUP)[None, None, :, :, None, None]
    p = sink_softmax(s, mask[None, :, None, None], sink)
    o = jnp.einsum('bchgqk,bckhd->bcqhgd', p.astype(vv.dtype), vv)
    return o.reshape(B, S, Q_WIDTH)


def swa_sample(q, k_new, v_new, cache_k, cache_v, sinks):
    Bd, n = q.shape[0], q.shape[1]
    W = cache_k.shape[1]
    kk = jnp.concatenate([cache_k.astype(k_new.dtype), k_new], axis=1)
    vv = jnp.concatenate([cache_v.astype(v_new.dtype), v_new], axis=1)
    qpos = PAST_LEN + jnp.arange(n)
    kpos = jnp.concatenate([PAST_LEN - W + jnp.arange(W), qpos])
    diff = qpos[:, None] - kpos[None, :]
    mask = (diff >= 0) & (diff <= WINDOW)
    qg = q.reshape(Bd, n, KV_HEADS, GROUP, HEAD_DIM)
    s = jnp.einsum('bqhgd,bkhd->bhgqk', qg, kk).astype(jnp.float32) * (HEAD_DIM ** -0.5)
    sink = sinks.reshape(KV_HEADS, GROUP)[None, :, :, None, None]
    p = sink_softmax(s, mask, sink)
    o = jnp.einsum('bhgqk,bkhd->bqhgd', p.astype(vv.dtype), vv).reshape(Bd, n, Q_WIDTH)
    return o, kk[:, -W:], vv[:, -W:]


def conv_ffn(h, prefix, w_up, conv_w, conv_b, w_down):
    S = h.shape[1]
    z = h @ w_up
    zc = jnp.concatenate([prefix.astype(z.dtype), z], axis=1)
    c = conv_b
    for t in range(CONV_W):
        c = c + conv_w[t] * zc[:, t:t + S]
    y = (jax.nn.gelu(c[..., :D_FF]) * c[..., D_FF:]) @ w_down
    return y, zc[:, -(CONV_W - 1):]


def decoder_layer(x, pos, wl, cache_k, cache_v, conv_state):
    (g_mix, w_in, ln_v_g, ln_v_b, w_s, b_s, sinks, w_pa, w_pb, w_o,
     g_ffn, w_up, conv_w, conv_b, w_down) = wl
    B, S = x.shape[0], x.shape[1]
    h = rms_norm(x, g_mix)
    u, vg, q, k, v, ga, gb = in_projection(h, w_in, ln_v_g, ln_v_b)
    q = rotary(q, pos)
    k = rotary(k, pos)
    y_a = spatial_gate(u, vg, w_s, b_s)
    last = S - ((S - 1) // CHUNK) * CHUNK
    gm_rows = vg[:, S - last:]
    if cache_k is None:
        y_b = swa_prompt(q, k, v, sinks)
        keep = min(WINDOW, S)
        new_k, new_v = k[:, S - keep:], v[:, S - keep:]
        prefix = jnp.zeros((B, CONV_W - 1, 2 * D_FF), dtype=x.dtype)
    else:
        y_b, new_k, new_v = swa_sample(q, k, v, cache_k, cache_v, sinks)
        prefix = conv_state
    merged = jax.nn.sigmoid(ga) * (y_a @ w_pa) + jax.nn.sigmoid(gb) * (y_b @ w_pb)
    x = x + merged @ w_o
    y_f, new_conv = conv_ffn(rms_norm(x, g_ffn), prefix, w_up, conv_w, conv_b, w_down)
    x = x + y_f
    return x, new_k, new_v, gm_rows, new_conv


def setup_inputs(seed: int = 0) -> dict:
    key = jax.random.key(seed)
    ks = jax.random.split(key, 24)
    f32 = jnp.float32
    w_buf = min(WINDOW, PAST_LEN)
    nrm = lambda k, shape, s: jax.random.normal(k, shape, f32) * s
    return {
        'x_prompt': nrm(ks[0], (BATCH, SEQ, D_MODEL), 1.0),
        'x_sample': nrm(ks[1], (DEC_BATCH, DEC_SEQ, D_MODEL), 1.0),
        'cache_swa_k': nrm(ks[2], (DEPTH, DEC_BATCH, w_buf, KV_HEADS, HEAD_DIM), 1.0),
        'cache_swa_v': nrm(ks[3], (DEPTH, DEC_BATCH, w_buf, KV_HEADS, HEAD_DIM), 1.0),
        'state_ffn_conv': nrm(ks[4], (DEPTH, DEC_BATCH, CONV_W - 1, 2 * D_FF), 1.0),
        'g_mix': 1.0 + nrm(ks[5], (DEPTH, D_MODEL), 0.05),
        'w_in': nrm(ks[6], (DEPTH, D_MODEL, IN_WIDTH), D_MODEL ** -0.5),
        'ln_v_g': 1.0 + nrm(ks[7], (DEPTH, GM_WIDTH), 0.05),
        'ln_v_b': nrm(ks[8], (DEPTH, GM_WIDTH), 0.02),
        'w_s': nrm(ks[9], (DEPTH, GM_HEADS, CHUNK, CHUNK), CHUNK ** -0.5),
        'b_s': 1.0 + nrm(ks[10], (DEPTH, GM_HEADS, CHUNK), 0.1),
        'sinks': nrm(ks[11], (DEPTH, N_HEADS), 0.5),
        'w_pa': nrm(ks[12], (DEPTH, GM_WIDTH, D_MODEL), GM_WIDTH ** -0.5),
        'w_pb': nrm(ks[13], (DEPTH, Q_WIDTH, D_MODEL), Q_WIDTH ** -0.5),
        'w_o': nrm(ks[14], (DEPTH, D_MODEL, D_MODEL), D_MODEL ** -0.5),
        'g_ffn': 1.0 + nrm(ks[15], (DEPTH, D_MODEL), 0.05),
        'w_up': nrm(ks[16], (DEPTH, D_MODEL, 2 * D_FF), D_MODEL ** -0.5),
        'conv_w': nrm(ks[17], (DEPTH, CONV_W, 2 * D_FF), CONV_W ** -0.5),
        'conv_b': nrm(ks[18], (DEPTH, 2 * D_FF), 0.02),
        'w_down': nrm(ks[19], (DEPTH, D_FF, D_MODEL), D_FF ** -0.5),
        'g_final': 1.0 + nrm(ks[20], (D_MODEL,), 0.05),
    }


def reference(x_prompt, x_sample, cache_swa_k, cache_swa_v, state_ffn_conv,
              g_mix, w_in, ln_v_g, ln_v_b, w_s, b_s, sinks, w_pa, w_pb, w_o,
              g_ffn, w_up, conv_w, conv_b, w_down, g_final):
    xp, xs = x_prompt, x_sample
    pos_p = jnp.arange(xp.shape[1], dtype=jnp.int32)
    pos_s = PAST_LEN + jnp.arange(xs.shape[1], dtype=jnp.int32)
    kp_l, vp_l, ap_l, cp_l = [], [], [], []
    ks_l, vs_l, as_l, cs_l = [], [], [], []
    for l in range(DEPTH):
        wl = (g_mix[l], w_in[l], ln_v_g[l], ln_v_b[l], w_s[l], b_s[l], sinks[l],
              w_pa[l], w_pb[l], w_o[l], g_ffn[l], w_up[l], conv_w[l], conv_b[l], w_down[l])
        xp, kp, vp, ap, cp = decoder_layer(xp, pos_p, wl, None, None, None)
        xs, ksn, vsn, asn, csn = decoder_layer(xs, pos_s, wl, cache_swa_k[l], cache_swa_v[l], state_ffn_conv[l])
        kp_l.append(kp); vp_l.append(vp); ap_l.append(ap); cp_l.append(cp)
        ks_l.append(ksn); vs_l.append(vsn); as_l.append(asn); cs_l.append(csn)
    y_prompt = rms_norm(xp, g_final)
    y_sample = rms_norm(xs, g_final)
    new_swa_k_prompt = jnp.stack(kp_l, axis=0)
    new_swa_v_prompt = jnp.stack(vp_l, axis=0)
    new_gm_v_prompt = jnp.stack(ap_l, axis=0)
    new_conv_prompt = jnp.stack(cp_l, axis=0)
    new_swa_k_sample = jnp.stack(ks_l, axis=0)
    new_swa_v_sample = jnp.stack(vs_l, axis=0)
    new_gm_v_sample = jnp.stack(as_l, axis=0)
    new_conv_sample = jnp.stack(cs_l, axis=0)
    return (y_prompt, y_sample, new_swa_k_prompt, new_swa_v_prompt, new_gm_v_prompt, new_conv_prompt,
            new_swa_k_sample, new_swa_v_sample, new_gm_v_sample, new_conv_sample)
```

```python
import functools
import math

import jax
import jax.numpy as jnp
from jax import lax
from jax.experimental import pallas as pl
from jax.experimental.pallas import tpu as pltpu

F32 = jnp.float32
BF16 = jnp.bfloat16

D_MODEL = 1024
SEQ = 2048
PAST_LEN = 16384
CHUNK = 128
GM_HEADS = 8
GM_WIDTH = 512
N_HEADS = 8
KV_HEADS = 2
HEAD_DIM = 64
Q_WIDTH = N_HEADS * HEAD_DIM
KV_WIDTH = KV_HEADS * HEAD_DIM
D_FF = 3 * D_MODEL
CONV_W = 3
EPS = 1e-6
ROPE_THETA = 10000.0
LANES = 128
SUBLANES = 8
OFF_Q = 2 * GM_WIDTH
OFF_K = OFF_Q + Q_WIDTH
OFF_V = OFF_K + KV_WIDTH
OFF_GA = OFF_V + KV_WIDTH
OFF_GB = OFF_GA + D_MODEL
IN_WIDTH = OFF_GB + D_MODEL

PROMPT_TILE = 256
SAMPLE_CHUNK = 16
FF_CHUNK = 512
VMEM_LIMIT = 52 * 1024 * 1024

_GELU_C0 = math.sqrt(2.0 / math.pi)
_GELU_C1 = _GELU_C0 * 0.044715


def _gelu(x):
    inner = x * (_GELU_C0 + _GELU_C1 * (x * x))
    hx = 0.5 * x
    return hx + hx * jnp.tanh(inner)


def _rms(x, g):
    ms = jnp.mean(x * x, axis=-1, keepdims=True)
    return x * lax.rsqrt(ms + EPS) * g


def _layer_norm(x, g, b):
    mu = jnp.mean(x, axis=-1, keepdims=True)
    xc = x - mu
    var = jnp.mean(xc * xc, axis=-1, keepdims=True)
    return xc * lax.rsqrt(var + EPS) * g + b


def _sigmoid(x):
    return 1.0 / (1.0 + jnp.exp(-x))


def _dot(a, b):
    return jnp.dot(a, b, preferred_element_type=F32)


def _dot_nt(a, b):
    return lax.dot_general(a, b, (((1,), (1,)), ((), ())), preferred_element_type=F32)


def _rope(xc, cos, sin_signed, first_half):
    partner = jnp.where(first_half, pltpu.roll(xc, LANES - 32, 1), pltpu.roll(xc, 32, 1))
    return xc * cos + partner * sin_signed


def _rope_tables(pos_f32, inv_row, shape):
    lane = lax.broadcasted_iota(jnp.int32, shape, 1)
    ang = pos_f32 * inv_row
    sin = jnp.sin(ang)
    return jnp.cos(ang), jnp.where((lane % HEAD_DIM) < HEAD_DIM // 2, -sin, sin)


def _prompt_mixer_kernel(sinks_ref, x_ref, inv_ref, g_ref, win_ref, lng_ref, lnb_ref, ws_ref, bs_ref,
                         wpa_ref, wpb_ref, wo_ref,
                         x1_ref, klast_ref, vlast_ref, gmlast_ref,
                         cos_ref, sin_ref, wcat_ref, kprev_ref, vprev_ref, *, tile):
    b = pl.program_id(0)
    s = pl.program_id(1)
    nblk = tile // CHUNK

    @pl.when((b == 0) & (s == 0))
    def _init_tables():
        def body(i, carry):
            r0 = pl.multiple_of(i * CHUNK, CHUNK)
            pos = (lax.broadcasted_iota(jnp.int32, (CHUNK, LANES), 0) + r0).astype(F32)
            cos, sin_signed = _rope_tables(pos, inv_ref[...], (CHUNK, LANES))
            cos_ref[pl.ds(r0, CHUNK), :] = cos
            sin_ref[pl.ds(r0, CHUNK), :] = sin_signed
            return carry
        lax.fori_loop(0, SEQ // CHUNK, body, 0)
        row = lax.broadcasted_iota(jnp.int32, (CHUNK, CHUNK), 0)
        col = lax.broadcasted_iota(jnp.int32, (CHUNK, CHUNK), 1)
        causal = col <= row
        for j in range(GM_HEADS // 2):
            wcat_ref[j, :, 0:CHUNK] = jnp.where(causal, ws_ref[2 * j], 0.0).astype(BF16)
            wcat_ref[j, :, CHUNK:2 * CHUNK] = jnp.where(causal, ws_ref[2 * j + 1], 0.0).astype(BF16)

    @pl.when(s == 0)
    def _reset_carry():
        kprev_ref[...] = jnp.zeros_like(kprev_ref)
        vprev_ref[...] = jnp.zeros_like(vprev_ref)

    lane = lax.broadcasted_iota(jnp.int32, (CHUNK, LANES), 1)
    low = lane < HEAD_DIM
    first_half_t = (lax.broadcasted_iota(jnp.int32, (tile, LANES), 1) % HEAD_DIM) < HEAD_DIM // 2

    x = x_ref[...]
    h = _rms(x, g_ref[...]).astype(BF16)

    a = _gelu(_dot(h, win_ref[:, 0:OFF_Q]))
    u = a[:, 0:GM_WIDTH]
    vg = _layer_norm(a[:, GM_WIDTH:OFF_Q], lng_ref[...], lnb_ref[...])
    ya_blocks = []
    for c in range(nblk):
        r = slice(c * CHUNK, (c + 1) * CHUNK)
        cols = []
        for j in range(GM_HEADS // 2):
            vcol = vg[r, j * LANES:(j + 1) * LANES]
            rhs = jnp.concatenate([jnp.where(low, vcol, 0.0), jnp.where(low, 0.0, vcol)], axis=0).astype(BF16)
            cols.append(_dot(wcat_ref[j], rhs))
        mix = jnp.concatenate(cols, axis=1) + bs_ref[...]
        ya_blocks.append((u[r] * mix).astype(BF16))
    ya = jnp.concatenate(ya_blocks, axis=0)

    zqkv = _dot(h, win_ref[:, OFF_Q:OFF_GA])
    r0 = pl.multiple_of(s * tile, tile)
    cos = cos_ref[pl.ds(r0, tile), :]
    sin_signed = sin_ref[pl.ds(r0, tile), :]
    qcols = [(_rope(zqkv[:, j * LANES:(j + 1) * LANES], cos, sin_signed, first_half_t)
              * (HEAD_DIM ** -0.5)).astype(BF16) for j in range(Q_WIDTH // LANES)]
    krot = _rope(zqkv[:, Q_WIDTH:Q_WIDTH + KV_WIDTH], cos, sin_signed, first_half_t)
    vval = zqkv[:, Q_WIDTH + KV_WIDTH:Q_WIDTH + 2 * KV_WIDTH]

    row = lax.broadcasted_iota(jnp.int32, (CHUNK, CHUNK), 0)
    col = lax.broadcasted_iota(jnp.int32, (CHUNK, CHUNK), 1)
    cur_mask = col <= row
    prev_live = (col - row) >= jnp.where(s > 0, 0, 2 * CHUNK)
    prev_inner = col >= row

    kp = kprev_ref[...]
    vp = vprev_ref[...]
    yb_blocks = []
    for i in range(nblk):
        r = slice(i * CHUNK, (i + 1) * CHUNK)
        kc = krot[r]
        vc = vval[r]
        kk = jnp.concatenate([kp, kc], axis=0)
        vv = jnp.concatenate([vp, vc], axis=0)
        low2 = lax.broadcasted_iota(jnp.int32, (2 * CHUNK, LANES), 1) < HEAD_DIM
        kk_sw = pltpu.roll(kk, HEAD_DIM, 1)
        vv_sw = pltpu.roll(vv, HEAD_DIM, 1)
        pmask = prev_live if i == 0 else prev_inner
        cols_out = []
        for g in range(KV_HEADS):
            k_src_lo = kk if g == 0 else kk_sw
            k_src_hi = kk_sw if g == 0 else kk
            v_src_lo = vv if g == 0 else vv_sw
            v_src_hi = vv_sw if g == 0 else vv
            k_lo = jnp.where(low2, k_src_lo, 0.0).astype(BF16)
            k_hi = jnp.where(low2, 0.0, k_src_hi).astype(BF16)
            v_lo = jnp.where(low2, v_src_lo, 0.0).astype(BF16)
            v_hi = jnp.where(low2, 0.0, v_src_hi).astype(BF16)
            krhs = jnp.concatenate([k_lo, k_hi], axis=0)
            vblk = jnp.concatenate([v_lo, v_hi], axis=0)
            qstack = jnp.concatenate([qcols[2 * g][r], qcols[2 * g + 1][r]], axis=0)
            sc = _dot_nt(qstack, krhs)
            p_rows = []
            rinv_rows = []
            for jj in range(2):
                ps = []
                ls = []
                for par in range(2):
                    hd = 4 * g + 2 * jj + par
                    sh = sc[jj * CHUNK:(jj + 1) * CHUNK, par * 2 * CHUNK:(par + 1) * 2 * CHUNK]
                    sp = jnp.where(pmask, sh[:, 0:CHUNK], -jnp.inf)
                    scur = jnp.where(cur_mask, sh[:, CHUNK:2 * CHUNK], -jnp.inf)
                    sink = sinks_ref[hd]
                    m = jnp.maximum(jnp.max(jnp.maximum(sp, scur), axis=-1, keepdims=True), sink)
                    pp = jnp.exp(sp - m)
                    pc = jnp.exp(scur - m)
                    lsum = jnp.sum(pp + pc, axis=-1, keepdims=True) + jnp.exp(sink - m)
                    ps.append(pp.astype(BF16))
                    ps.append(pc.astype(BF16))
                    ls.append(lsum)
                p_rows.append(jnp.concatenate(ps, axis=1))
                rinv_rows.append(jnp.where(low, 1.0 / ls[0], 1.0 / ls[1]))
            o = _dot(jnp.concatenate(p_rows, axis=0), vblk)
            cols_out.append(o[0:CHUNK] * rinv_rows[0])
            cols_out.append(o[CHUNK:2 * CHUNK] * rinv_rows[1])
        yb_blocks.append(jnp.concatenate(cols_out, axis=1).astype(BF16))
        kp = kc
        vp = vc
    yb = jnp.concatenate(yb_blocks, axis=0)
    kprev_ref[...] = kp
    vprev_ref[...] = vp

    @pl.when(s == pl.num_programs(1) - 1)
    def _emit_state():
        klast_ref[...] = krot[tile - CHUNK:tile]
        vlast_ref[...] = vval[tile - CHUNK:tile]
        gmlast_ref[...] = vg[tile - CHUNK:tile]

    gates = _dot(h, win_ref[:, OFF_GA:IN_WIDTH])
    merged = (_sigmoid(gates[:, 0:D_MODEL]) * _dot(ya, wpa_ref[...])
              + _sigmoid(gates[:, D_MODEL:2 * D_MODEL]) * _dot(yb, wpb_ref[...]))
    x1_ref[...] = x + _dot(merged.astype(BF16), wo_ref[...])


def _const_spec(shape):
    nd = len(shape)
    return pl.BlockSpec(shape, lambda *_: (0,) * nd, pipeline_mode=pl.Buffered(1))


def _prompt_mixer(x, sinks, inv_row, g_mix, w_in, ln_g, ln_b, w_s, bs_full, w_pa, w_pb, w_o):
    nb, seq, _ = x.shape
    tile = PROMPT_TILE
    grid = (nb, seq // tile)
    row_spec = pl.BlockSpec((None, tile, D_MODEL), lambda b, s: (b, s, 0))
    last = lambda w: pl.BlockSpec((None, CHUNK, w), lambda b, s: (b, 0, 0))
    return pl.pallas_call(
        functools.partial(_prompt_mixer_kernel, tile=tile),
        grid=grid,
        in_specs=[
            pl.BlockSpec(memory_space=pltpu.SMEM),
            row_spec,
            _const_spec((1, LANES)),
            _const_spec((1, D_MODEL)),
            _const_spec((D_MODEL, IN_WIDTH)),
            _const_spec((1, GM_WIDTH)),
            _const_spec((1, GM_WIDTH)),
            _const_spec((GM_HEADS, CHUNK, CHUNK)),
            _const_spec((CHUNK, GM_WIDTH)),
            _const_spec((GM_WIDTH, D_MODEL)),
            _const_spec((Q_WIDTH, D_MODEL)),
            _const_spec((D_MODEL, D_MODEL)),
        ],
        out_specs=[row_spec, last(KV_WIDTH), last(KV_WIDTH), last(GM_WIDTH)],
        out_shape=[
            jax.ShapeDtypeStruct((nb, seq, D_MODEL), F32),
            jax.ShapeDtypeStruct((nb, CHUNK, KV_WIDTH), F32),
            jax.ShapeDtypeStruct((nb, CHUNK, KV_WIDTH), F32),
            jax.ShapeDtypeStruct((nb, CHUNK, GM_WIDTH), F32),
        ],
        scratch_shapes=[
            pltpu.VMEM((SEQ, LANES), F32),
            pltpu.VMEM((SEQ, LANES), F32),
            pltpu.VMEM((GM_HEADS // 2, CHUNK, 2 * CHUNK), BF16),
            pltpu.VMEM((CHUNK, KV_WIDTH), F32),
            pltpu.VMEM((CHUNK, KV_WIDTH), F32),
        ],
        compiler_params=pltpu.CompilerParams(
            dimension_semantics=("arbitrary", "arbitrary"), vmem_limit_bytes=VMEM_LIMIT),
        name="prompt_mixer",
    )(sinks, x, inv_row, g_mix, w_in, ln_g, ln_b, w_s, bs_full, w_pa, w_pb, w_o)


def _prompt_ffn_kernel(x_ref, g_ref, wup_ref, cw_ref, cb_ref, wdn_ref, gf_ref,
                       y_ref, convlast_ref, zbuf_ref, gbuf_ref, *, tile):
    s = pl.program_id(1)
    pad = SUBLANES

    @pl.when(s == 0)
    def _reset_carry():
        zbuf_ref[0:pad, :] = jnp.zeros((pad, 2 * D_FF), F32)

    x = x_ref[...]
    h = _rms(x, g_ref[...]).astype(BF16)
    nchunk = D_FF // FF_CHUNK
    for j in range(nchunk):
        for half in range(2):
            c0 = half * D_FF + j * FF_CHUNK
            zbuf_ref[pad:pad + tile, c0:c0 + FF_CHUNK] = _dot(h, wup_ref[:, c0:c0 + FF_CHUNK])
    for j in range(nchunk):
        conv = []
        for half in range(2):
            c0 = half * D_FF + j * FF_CHUNK
            cs = slice(c0, c0 + FF_CHUNK)
            conv.append(cb_ref[:, cs]
                        + cw_ref[0:1, cs] * zbuf_ref[pad - 2:pad - 2 + tile, cs]
                        + cw_ref[1:2, cs] * zbuf_ref[pad - 1:pad - 1 + tile, cs]
                        + cw_ref[2:3, cs] * zbuf_ref[pad:pad + tile, cs])
        gbuf_ref[:, j * FF_CHUNK:(j + 1) * FF_CHUNK] = (_gelu(conv[0]) * conv[1]).astype(BF16)

    @pl.when(s == pl.num_programs(1) - 1)
    def _emit_state():
        convlast_ref[...] = zbuf_ref[pad + tile - (CONV_W - 1):pad + tile, :]

    zbuf_ref[0:pad, :] = zbuf_ref[tile:tile + pad, :]
    x2 = x + _dot(gbuf_ref[...], wdn_ref[...])
    y_ref[...] = _rms(x2, gf_ref[...])


def _prompt_ffn(x1, g_ffn, w_up, conv_w, conv_b, w_down, g_final):
    nb, seq, _ = x1.shape
    tile = PROMPT_TILE
    row_spec = pl.BlockSpec((None, tile, D_MODEL), lambda b, s: (b, s, 0))
    return pl.pallas_call(
        functools.partial(_prompt_ffn_kernel, tile=tile),
        grid=(nb, seq // tile),
        in_specs=[
            row_spec,
            _const_spec((1, D_MODEL)),
            _const_spec((D_MODEL, 2 * D_FF)),
            _const_spec((CONV_W, 2 * D_FF)),
            _const_spec((1, 2 * D_FF)),
            _const_spec((D_FF, D_MODEL)),
            _const_spec((1, D_MODEL)),
        ],
        out_specs=[row_spec, pl.BlockSpec((None, CONV_W - 1, 2 * D_FF), lambda b, s: (b, 0, 0))],
        out_shape=[
            jax.ShapeDtypeStruct((nb, seq, D_MODEL), F32),
            jax.ShapeDtypeStruct((nb, CONV_W - 1, 2 * D_FF), F32),
        ],
        scratch_shapes=[
            pltpu.VMEM((tile + SUBLANES, 2 * D_FF), F32),
            pltpu.VMEM((tile, D_FF), BF16),
        ],
        compiler_params=pltpu.CompilerParams(
            dimension_semantics=("arbitrary", "arbitrary"), vmem_limit_bytes=VMEM_LIMIT),
        name="prompt_ffn",
    )(x1, g_ffn, w_up, conv_w, conv_b, w_down, g_final)


def _sample_mixer_kernel(sinks_ref, x_ref, ck_ref, cv_ref, inv_ref, g_ref, win_ref, lng_ref, lnb_ref,
                         ws0_ref, bs0_ref, wpa_ref, wpb_ref, wo_ref,
                         x1_ref, gm_ref, ckn_ref, cvn_ref,
                         qz_ref, kn_ref, vn_ref, o_ref, *, chunk):
    lane = lax.broadcasted_iota(jnp.int32, (chunk, LANES), 1)
    low = lane < HEAD_DIM
    first_half = (lane % HEAD_DIM) < HEAD_DIM // 2

    x = x_ref[...]
    h = _rms(x, g_ref[...]).astype(BF16)
    a = _gelu(_dot(h, win_ref[:, 0:OFF_Q]))
    u = a[:, 0:GM_WIDTH]
    vg = _layer_norm(a[:, GM_WIDTH:OFF_Q], lng_ref[...], lnb_ref[...])
    gm_ref[...] = vg
    ya = (u * (vg * ws0_ref[...] + bs0_ref[...])).astype(BF16)

    zqkv = _dot(h, win_ref[:, OFF_Q:OFF_GA])
    pos = jnp.full((chunk, LANES), float(PAST_LEN), F32)
    cos, sin_signed = _rope_tables(pos, inv_ref[...], (chunk, LANES))
    knew = _rope(zqkv[:, Q_WIDTH:Q_WIDTH + KV_WIDTH], cos, sin_signed, first_half)
    vnew = zqkv[:, Q_WIDTH + KV_WIDTH:Q_WIDTH + 2 * KV_WIDTH]
    kn_ref[...] = knew
    vn_ref[...] = vnew
    for j in range(Q_WIDTH // LANES):
        qc = _rope(zqkv[:, j * LANES:(j + 1) * LANES], cos, sin_signed, first_half) * (HEAD_DIM ** -0.5)
        qsw = pltpu.roll(qc, HEAD_DIM, 1)
        g = j // 2
        even = jnp.where(low, qc, 0.0) if g == 0 else jnp.where(low, 0.0, qsw)
        odd = jnp.where(low, qsw, 0.0) if g == 0 else jnp.where(low, 0.0, qc)
        qz_ref[(2 * j) * chunk:(2 * j + 1) * chunk, :] = even
        qz_ref[(2 * j + 1) * chunk:(2 * j + 2) * chunk, :] = odd

    hrow = lax.broadcasted_iota(jnp.int32, (N_HEADS, 1), 0)
    sink_col = jnp.zeros((N_HEADS, 1), F32)
    for hd in range(N_HEADS):
        sink_col = jnp.where(hrow == hd, sinks_ref[hd], sink_col)
    head_low = lax.broadcasted_iota(jnp.int32, (N_HEADS, LANES), 0) < N_HEADS // KV_HEADS
    lane_low = lax.broadcasted_iota(jnp.int32, (N_HEADS, LANES), 1) < HEAD_DIM
    own_half = head_low == lane_low
    krow = lax.broadcasted_iota(jnp.int32, (CHUNK, LANES), 0)

    def body(bi, carry):
        qz = qz_ref[pl.ds(bi, N_HEADS, stride=chunk), :]
        kb = ck_ref[bi]
        vb = cv_ref[bi]
        kn = kn_ref[pl.ds(bi, 1), :]
        vn = vn_ref[pl.ds(bi, 1), :]
        sc = _dot_nt(qz.astype(BF16), kb.astype(BF16))
        sn = jnp.sum(qz * kn, axis=-1, keepdims=True)
        m = jnp.maximum(jnp.maximum(jnp.max(sc, axis=-1, keepdims=True), sn), sink_col)
        p = jnp.exp(sc - m)
        pn = jnp.exp(sn - m)
        lsum = jnp.sum(p, axis=-1, keepdims=True) + pn + jnp.exp(sink_col - m)
        o = (_dot(p.astype(BF16), vb.astype(BF16)) + pn * vn) / lsum
        o_ref[pl.ds(bi, N_HEADS, stride=chunk), :] = jnp.where(own_half, o, 0.0)
        ckn_ref[bi] = jnp.where(krow == CHUNK - 1, kn, pltpu.roll(kb, CHUNK - 1, 0))
        cvn_ref[bi] = jnp.where(krow == CHUNK - 1, vn, pltpu.roll(vb, CHUNK - 1, 0))
        return carry

    lax.fori_loop(0, chunk, body, 0)

    cols = []
    for j in range(Q_WIDTH // LANES):
        g = j // 2
        oe = o_ref[(2 * j) * chunk:(2 * j + 1) * chunk, :]
        oo = o_ref[(2 * j + 1) * chunk:(2 * j + 2) * chunk, :]
        if g == 0:
            cols.append(jnp.where(low, oe, pltpu.roll(oo, HEAD_DIM, 1)))
        else:
            cols.append(jnp.where(low, pltpu.roll(oe, HEAD_DIM, 1), oo))
    yb = jnp.concatenate(cols, axis=1).astype(BF16)

    gates = _dot(h, win_ref[:, OFF_GA:IN_WIDTH])
    merged = (_sigmoid(gates[:, 0:D_MODEL]) * _dot(ya, wpa_ref[...])
              + _sigmoid(gates[:, D_MODEL:2 * D_MODEL]) * _dot(yb, wpb_ref[...]))
    x1_ref[...] = x + _dot(merged.astype(BF16), wo_ref[...])


def _sample_mixer(x, ck, cv, sinks, inv_row, g_mix, w_in, ln_g, ln_b, ws0, bs0, w_pa, w_pb, w_o):
    n = x.shape[0]
    chunk = SAMPLE_CHUNK
    row = lambda w: pl.BlockSpec((chunk, w), lambda i: (i, 0))
    cache = pl.BlockSpec((chunk, CHUNK, KV_WIDTH), lambda i: (i, 0, 0))
    return pl.pallas_call(
        functools.partial(_sample_mixer_kernel, chunk=chunk),
        grid=(n // chunk,),
        in_specs=[
            pl.BlockSpec(memory_space=pltpu.SMEM),
            row(D_MODEL), cache, cache,
            _const_spec((1, LANES)),
            _const_spec((1, D_MODEL)),
            _const_spec((D_MODEL, IN_WIDTH)),
            _const_spec((1, GM_WIDTH)),
            _const_spec((1, GM_WIDTH)),
            _const_spec((1, GM_WIDTH)),
            _const_spec((1, GM_WIDTH)),
            _const_spec((GM_WIDTH, D_MODEL)),
            _const_spec((Q_WIDTH, D_MODEL)),
            _const_spec((D_MODEL, D_MODEL)),
        ],
        out_specs=[row(D_MODEL), row(GM_WIDTH), cache, cache],
        out_shape=[
            jax.ShapeDtypeStruct((n, D_MODEL), F32),
            jax.ShapeDtypeStruct((n, GM_WIDTH), F32),
            jax.ShapeDtypeStruct((n, CHUNK, KV_WIDTH), F32),
            jax.ShapeDtypeStruct((n, CHUNK, KV_WIDTH), F32),
        ],
        scratch_shapes=[
            pltpu.VMEM((N_HEADS * chunk, LANES), F32),
            pltpu.VMEM((chunk, KV_WIDTH), F32),
            pltpu.VMEM((chunk, KV_WIDTH), F32),
            pltpu.VMEM((N_HEADS * chunk, LANES), F32),
        ],
        compiler_params=pltpu.CompilerParams(
            dimension_semantics=("arbitrary",), vmem_limit_bytes=VMEM_LIMIT),
        name="sample_mixer",
    )(sinks, x, ck, cv, inv_row, g_mix, w_in, ln_g, ln_b, ws0, bs0, w_pa, w_pb, w_o)


def _sample_ffn_kernel(x_ref, g_ref, wua_ref, wub_ref, cwa_ref, cwb_ref, cba_ref, cbb_ref,
                       p0a_ref, p0b_ref, p1a_ref, p1b_ref, wdn_ref, gf_ref,
                       y_ref, za_ref, zb_ref, acc_ref):
    j = pl.program_id(0)
    x = x_ref[...]
    h = _rms(x, g_ref[...]).astype(BF16)
    za = _dot(h, wua_ref[...])
    zb = _dot(h, wub_ref[...])
    za_ref[...] = za
    zb_ref[...] = zb
    ca = cba_ref[...] + cwa_ref[0:1, :] * p0a_ref[...] + cwa_ref[1:2, :] * p1a_ref[...] + cwa_ref[2:3, :] * za
    cb = cbb_ref[...] + cwb_ref[0:1, :] * p0b_ref[...] + cwb_ref[1:2, :] * p1b_ref[...] + cwb_ref[2:3, :] * zb
    part = _dot((_gelu(ca) * cb).astype(BF16), wdn_ref[...])

    @pl.when(j == 0)
    def _first():
        acc_ref[...] = part

    @pl.when(j > 0)
    def _rest():
        acc_ref[...] += part

    @pl.when(j == pl.num_programs(0) - 1)
    def _finish():
        y_ref[...] = _rms(x + acc_ref[...], gf_ref[...])


def _sample_ffn(x1, state2d, g_ffn, w_up, conv_w, conv_b, w_down, g_final):
    n = x1.shape[0]
    nchunk = D_FF // FF_CHUNK
    full = lambda w: pl.BlockSpec((n, w), lambda j: (0, 0))
    col = lambda rows, off: pl.BlockSpec((rows, FF_CHUNK), lambda j, off=off: (0, j + off))
    return pl.pallas_call(
        _sample_ffn_kernel,
        grid=(nchunk,),
        in_specs=[
            full(D_MODEL),
            pl.BlockSpec((1, D_MODEL), lambda j: (0, 0)),
            col(D_MODEL, 0), col(D_MODEL, nchunk),
            col(CONV_W, 0), col(CONV_W, nchunk),
            col(1, 0), col(1, nchunk),
            col(n, 0), col(n, nchunk), col(n, 2 * nchunk), col(n, 3 * nchunk),
            pl.BlockSpec((FF_CHUNK, D_MODEL), lambda j: (j, 0)),
            pl.BlockSpec((1, D_MODEL), lambda j: (0, 0)),
        ],
        out_specs=[full(D_MODEL), col(n, 0), col(n, 0)],
        out_shape=[
            jax.ShapeDtypeStruct((n, D_MODEL), F32),
            jax.ShapeDtypeStruct((n, D_FF), F32),
            jax.ShapeDtypeStruct((n, D_FF), F32),
        ],
        scratch_shapes=[pltpu.VMEM((n, D_MODEL), F32)],
        compiler_params=pltpu.CompilerParams(
            dimension_semantics=("arbitrary",), vmem_limit_bytes=VMEM_LIMIT),
        name="sample_ffn",
    )(x1, g_ffn, w_up, w_up, conv_w, conv_w, conv_b, conv_b,
      state2d, state2d, state2d, state2d, w_down, g_final)


def kernel(x_prompt, x_sample, cache_swa_k, cache_swa_v, state_ffn_conv, g_mix, w_in, ln_v_g, ln_v_b,
           w_s, b_s, sinks, w_pa, w_pb, w_o, g_ffn, w_up, conv_w, conv_b, w_down, g_final):
    depth = g_mix.shape[0]
    assert depth == 1
    nb = x_prompt.shape[0]
    nd = x_sample.shape[0]
    half = HEAD_DIM // 2
    inv = ROPE_THETA ** (-jnp.arange(half, dtype=F32) / half)
    inv_row = jnp.tile(inv, LANES // half)[None, :]

    l = 0
    row = lambda v: v[l][None, :]
    w_in_b = w_in[l].astype(BF16)
    w_pa_b = w_pa[l].astype(BF16)
    w_pb_b = w_pb[l].astype(BF16)
    w_o_b = w_o[l].astype(BF16)
    w_up_b = w_up[l].astype(BF16)
    w_dn_b = w_down[l].astype(BF16)
    bs_full = jnp.repeat(b_s[l].T, HEAD_DIM, axis=1)
    ws0 = jnp.repeat(w_s[l][:, 0, 0], HEAD_DIM)[None, :]
    bs0 = bs_full[0:1]

    x1p, kp, vp, gmp = _prompt_mixer(x_prompt, sinks[l], inv_row, row(g_mix), w_in_b, row(ln_v_g), row(ln_v_b),
                                     w_s[l], bs_full, w_pa_b, w_pb_b, w_o_b)
    yp, convp = _prompt_ffn(x1p, row(g_ffn), w_up_b, conv_w[l], row(conv_b), w_dn_b, g_final[None, :])

    xs = x_sample.reshape(nd, D_MODEL)
    ck = cache_swa_k[l].reshape(nd, CHUNK, KV_WIDTH)
    cv = cache_swa_v[l].reshape(nd, CHUNK, KV_WIDTH)
    x1s, gms, ckn, cvn = _sample_mixer(xs, ck, cv, sinks[l], inv_row, row(g_mix), w_in_b, row(ln_v_g),
                                       row(ln_v_b), ws0, bs0, w_pa_b, w_pb_b, w_o_b)
    state = state_ffn_conv[l]
    ys, za, zb = _sample_ffn(x1s, state.reshape(nd, (CONV_W - 1) * 2 * D_FF), row(g_ffn), w_up_b,
                             conv_w[l], row(conv_b), w_dn_b, g_final[None, :])
    conv_s = jnp.stack([state[:, 1, :], jnp.concatenate([za, zb], axis=1)], axis=1)

    return (yp,
            ys.reshape(nd, 1, D_MODEL),
            kp.reshape(1, nb, CHUNK, KV_HEADS, HEAD_DIM),
            vp.reshape(1, nb, CHUNK, KV_HEADS, HEAD_DIM),
            gmp.reshape(1, nb, CHUNK, GM_HEADS, GM_WIDTH // GM_HEADS),
            convp[None],
            ckn.reshape(1, nd, CHUNK, KV_HEADS, HEAD_DIM),
            cvn.reshape(1, nd, CHUNK, KV_HEADS, HEAD_DIM),
            gms.reshape(1, nd, 1, GM_HEADS, GM_WIDTH // GM_HEADS),
            conv_s[None])
```

```python
import functools
import math

import jax
import jax.numpy as jnp
from jax import lax
from jax.experimental import pallas as pl
from jax.experimental.pallas import tpu as pltpu

F32 = jnp.float32
BF16 = jnp.bfloat16

D_MODEL = 1024
SEQ = 2048
PAST_LEN = 16384
CHUNK = 128
GM_HEADS = 8
GM_WIDTH = 512
N_HEADS = 8
KV_HEADS = 2
HEAD_DIM = 64
Q_WIDTH = N_HEADS * HEAD_DIM
KV_WIDTH = KV_HEADS * HEAD_DIM
D_FF = 3 * D_MODEL
CONV_W = 3
EPS = 1e-6
ROPE_THETA = 10000.0
LANES = 128
SUBLANES = 8
OFF_Q = 2 * GM_WIDTH
OFF_K = OFF_Q + Q_WIDTH
OFF_V = OFF_K + KV_WIDTH
OFF_GA = OFF_V + KV_WIDTH
OFF_GB = OFF_GA + D_MODEL
IN_WIDTH = OFF_GB + D_MODEL

PROMPT_TILE = 256
SAMPLE_CHUNK = 16
FF_CHUNK = 512
ROW_PHASES = 4
VMEM_LIMIT = 52 * 1024 * 1024

_GELU_C0 = math.sqrt(2.0 / math.pi)
_GELU_C1 = _GELU_C0 * 0.044715


def _gelu(x):
    inner = x * (_GELU_C0 + _GELU_C1 * (x * x))
    hx = 0.5 * x
    return hx + hx * jnp.tanh(inner)


def _rms(x, g):
    ms = jnp.mean(x * x, axis=-1, keepdims=True)
    return x * lax.rsqrt(ms + EPS) * g


def _layer_norm(x, g, b):
    mu = jnp.mean(x, axis=-1, keepdims=True)
    xc = x - mu
    var = jnp.mean(xc * xc, axis=-1, keepdims=True)
    return xc * lax.rsqrt(var + EPS) * g + b


def _sigmoid(x):
    return 1.0 / (1.0 + jnp.exp(-x))


def _dot(a, b):
    return jnp.dot(a, b, preferred_element_type=F32)


def _dot_nt(a, b):
    return lax.dot_general(a, b, (((1,), (1,)), ((), ())), preferred_element_type=F32)


def _rope(xc, cos, sin_signed, first_half):
    partner = jnp.where(first_half, pltpu.roll(xc, LANES - 32, 1), pltpu.roll(xc, 32, 1))
    return xc * cos + partner * sin_signed


def _rope_tables(pos_f32, inv_row, shape):
    lane = lax.broadcasted_iota(jnp.int32, shape, 1)
    ang = pos_f32 * inv_row
    sin = jnp.sin(ang)
    return jnp.cos(ang), jnp.where((lane % HEAD_DIM) < HEAD_DIM // 2, -sin, sin)


def _prompt_mixer_kernel(sinks_ref, x_ref, inv_ref, g_ref, win_ref, lng_ref, lnb_ref, ws_ref, bs_ref,
                         wpa_ref, wpb_ref, wo_ref,
                         x1_ref, klast_ref, vlast_ref, gmlast_ref,
                         cos_ref, sin_ref, wcat_ref, kprev_ref, vprev_ref, *, tile):
    b = pl.program_id(0)
    s = pl.program_id(1)
    nblk = tile // CHUNK

    @pl.when((b == 0) & (s == 0))
    def _init_tables():
        def body(i, carry):
            r0 = pl.multiple_of(i * CHUNK, CHUNK)
            pos = (lax.broadcasted_iota(jnp.int32, (CHUNK, LANES), 0) + r0).astype(F32)
            cos, sin_signed = _rope_tables(pos, inv_ref[...], (CHUNK, LANES))
            cos_ref[pl.ds(r0, CHUNK), :] = cos
            sin_ref[pl.ds(r0, CHUNK), :] = sin_signed
            return carry
        lax.fori_loop(0, SEQ // CHUNK, body, 0)
        row = lax.broadcasted_iota(jnp.int32, (CHUNK, CHUNK), 0)
        col = lax.broadcasted_iota(jnp.int32, (CHUNK, CHUNK), 1)
        causal = col <= row
        for j in range(GM_HEADS // 2):
            wcat_ref[j, :, 0:CHUNK] = jnp.where(causal, ws_ref[2 * j], 0.0).astype(BF16)
            wcat_ref[j, :, CHUNK:2 * CHUNK] = jnp.where(causal, ws_ref[2 * j + 1], 0.0).astype(BF16)

    @pl.when(s == 0)
    def _reset_carry():
        kprev_ref[...] = jnp.zeros_like(kprev_ref)
        vprev_ref[...] = jnp.zeros_like(vprev_ref)

    lane = lax.broadcasted_iota(jnp.int32, (CHUNK, LANES), 1)
    low = lane < HEAD_DIM
    first_half_t = (lax.broadcasted_iota(jnp.int32, (tile, LANES), 1) % HEAD_DIM) < HEAD_DIM // 2

    x = x_ref[...]
    h = _rms(x, g_ref[...]).astype(BF16)

    a = _gelu(_dot(h, win_ref[:, 0:OFF_Q]))
    u = a[:, 0:GM_WIDTH]
    vg = _layer_norm(a[:, GM_WIDTH:OFF_Q], lng_ref[...], lnb_ref[...])
    ya_blocks = []
    for c in range(nblk):
        r = slice(c * CHUNK, (c + 1) * CHUNK)
        cols = []
        for j in range(GM_HEADS // 2):
            vcol = vg[r, j * LANES:(j + 1) * LANES]
            rhs = jnp.concatenate([jnp.where(low, vcol, 0.0), jnp.where(low, 0.0, vcol)], axis=0).astype(BF16)
            cols.append(_dot(wcat_ref[j], rhs))
        mix = jnp.concatenate(cols, axis=1) + bs_ref[...]
        ya_blocks.append((u[r] * mix).astype(BF16))
    ya = jnp.concatenate(ya_blocks, axis=0)

    zqkv = _dot(h, win_ref[:, OFF_Q:OFF_GA])
    r0 = pl.multiple_of(s * tile, tile)
    cos = cos_ref[pl.ds(r0, tile), :]
    sin_signed = sin_ref[pl.ds(r0, tile), :]
    qcols = [(_rope(zqkv[:, j * LANES:(j + 1) * LANES], cos, sin_signed, first_half_t)
              * (HEAD_DIM ** -0.5)).astype(BF16) for j in range(Q_WIDTH // LANES)]
    krot = _rope(zqkv[:, Q_WIDTH:Q_WIDTH + KV_WIDTH], cos, sin_signed, first_half_t)
    vval = zqkv[:, Q_WIDTH + KV_WIDTH:Q_WIDTH + 2 * KV_WIDTH]

    row = lax.broadcasted_iota(jnp.int32, (CHUNK, CHUNK), 0)
    col = lax.broadcasted_iota(jnp.int32, (CHUNK, CHUNK), 1)
    cur_mask = col <= row
    prev_live = (col - row) >= jnp.where(s > 0, 0, 2 * CHUNK)
    prev_inner = col >= row

    kp = kprev_ref[...]
    vp = vprev_ref[...]
    yb_blocks = []
    for i in range(nblk):
        r = slice(i * CHUNK, (i + 1) * CHUNK)
        kc = krot[r]
        vc = vval[r]
        kk = jnp.concatenate([kp, kc], axis=0)
        vv = jnp.concatenate([vp, vc], axis=0)
        low2 = lax.broadcasted_iota(jnp.int32, (2 * CHUNK, LANES), 1) < HEAD_DIM
        kk_sw = pltpu.roll(kk, HEAD_DIM, 1)
        vv_sw = pltpu.roll(vv, HEAD_DIM, 1)
        pmask = prev_live if i == 0 else prev_inner
        cols_out = []
        for g in range(KV_HEADS):
            k_src_lo = kk if g == 0 else kk_sw
            k_src_hi = kk_sw if g == 0 else kk
            v_src_lo = vv if g == 0 else vv_sw
            v_src_hi = vv_sw if g == 0 else vv
            k_lo = jnp.where(low2, k_src_lo, 0.0).astype(BF16)
            k_hi = jnp.where(low2, 0.0, k_src_hi).astype(BF16)
            v_lo = jnp.where(low2, v_src_lo, 0.0).astype(BF16)
            v_hi = jnp.where(low2, 0.0, v_src_hi).astype(BF16)
            krhs = jnp.concatenate([k_lo, k_hi], axis=0)
            vblk = jnp.concatenate([v_lo, v_hi], axis=0)
            qstack = jnp.concatenate([qcols[2 * g][r], qcols[2 * g + 1][r]], axis=0)
            sc = _dot_nt(qstack, krhs)
            p_rows = []
            rinv_rows = []
            for jj in range(2):
                ps = []
                ls = []
                for par in range(2):
                    hd = 4 * g + 2 * jj + par
                    sh = sc[jj * CHUNK:(jj + 1) * CHUNK, par * 2 * CHUNK:(par + 1) * 2 * CHUNK]
                    sp = jnp.where(pmask, sh[:, 0:CHUNK], -jnp.inf)
                    scur = jnp.where(cur_mask, sh[:, CHUNK:2 * CHUNK], -jnp.inf)
                    sink = sinks_ref[hd]
                    m = jnp.maximum(jnp.max(jnp.maximum(sp, scur), axis=-1, keepdims=True), sink)
                    pp = jnp.exp(sp - m)
                    pc = jnp.exp(scur - m)
                    lsum = jnp.sum(pp + pc, axis=-1, keepdims=True) + jnp.exp(sink - m)
                    ps.append(pp.astype(BF16))
                    ps.append(pc.astype(BF16))
                    ls.append(lsum)
                p_rows.append(jnp.concatenate(ps, axis=1))
                rinv_rows.append(jnp.where(low, 1.0 / ls[0], 1.0 / ls[1]))
            o = _dot(jnp.concatenate(p_rows, axis=0), vblk)
            cols_out.append(o[0:CHUNK] * rinv_rows[0])
            cols_out.append(o[CHUNK:2 * CHUNK] * rinv_rows[1])
        yb_blocks.append(jnp.concatenate(cols_out, axis=1).astype(BF16))
        kp = kc
        vp = vc
    yb = jnp.concatenate(yb_blocks, axis=0)
    kprev_ref[...] = kp
    vprev_ref[...] = vp

    @pl.when(s == pl.num_programs(1) - 1)
    def _emit_state():
        klast_ref[...] = krot[tile - CHUNK:tile]
        vlast_ref[...] = vval[tile - CHUNK:tile]
        gmlast_ref[...] = vg[tile - CHUNK:tile]

    gates = _dot(h, win_ref[:, OFF_GA:IN_WIDTH])
    merged = (_sigmoid(gates[:, 0:D_MODEL]) * _dot(ya, wpa_ref[...])
              + _sigmoid(gates[:, D_MODEL:2 * D_MODEL]) * _dot(yb, wpb_ref[...]))
    x1_ref[...] = x + _dot(merged.astype(BF16), wo_ref[...])


def _const_spec(shape):
    nd = len(shape)
    return pl.BlockSpec(shape, lambda *_: (0,) * nd, pipeline_mode=pl.Buffered(1))


def _prompt_mixer(x, sinks, inv_row, g_mix, w_in, ln_g, ln_b, w_s, bs_full, w_pa, w_pb, w_o):
    nb, seq, _ = x.shape
    tile = PROMPT_TILE
    grid = (nb, seq // tile)
    row_spec = pl.BlockSpec((None, tile, D_MODEL), lambda b, s: (b, s, 0))
    last = lambda w: pl.BlockSpec((None, CHUNK, w), lambda b, s: (b, 0, 0))
    return pl.pallas_call(
        functools.partial(_prompt_mixer_kernel, tile=tile),
        grid=grid,
        in_specs=[
            pl.BlockSpec(memory_space=pltpu.SMEM),
            row_spec,
            _const_spec((1, LANES)),
            _const_spec((1, D_MODEL)),
            _const_spec((D_MODEL, IN_WIDTH)),
            _const_spec((1, GM_WIDTH)),
            _const_spec((1, GM_WIDTH)),
            _const_spec((GM_HEADS, CHUNK, CHUNK)),
            _const_spec((CHUNK, GM_WIDTH)),
            _const_spec((GM_WIDTH, D_MODEL)),
            _const_spec((Q_WIDTH, D_MODEL)),
            _const_spec((D_MODEL, D_MODEL)),
        ],
        out_specs=[row_spec, last(KV_WIDTH), last(KV_WIDTH), last(GM_WIDTH)],
        out_shape=[
            jax.ShapeDtypeStruct((nb, seq, D_MODEL), F32),
            jax.ShapeDtypeStruct((nb, CHUNK, KV_WIDTH), F32),
            jax.ShapeDtypeStruct((nb, CHUNK, KV_WIDTH), F32),
            jax.ShapeDtypeStruct((nb, CHUNK, GM_WIDTH), F32),
        ],
        scratch_shapes=[
            pltpu.VMEM((SEQ, LANES), F32),
            pltpu.VMEM((SEQ, LANES), F32),
            pltpu.VMEM((GM_HEADS // 2, CHUNK, 2 * CHUNK), BF16),
            pltpu.VMEM((CHUNK, KV_WIDTH), F32),
            pltpu.VMEM((CHUNK, KV_WIDTH), F32),
        ],
        compiler_params=pltpu.CompilerParams(
            dimension_semantics=("arbitrary", "arbitrary"), vmem_limit_bytes=VMEM_LIMIT),
        name="prompt_mixer",
    )(sinks, x, inv_row, g_mix, w_in, ln_g, ln_b, w_s, bs_full, w_pa, w_pb, w_o)


def _prompt_ffn_kernel(x_ref, g_ref, wup_ref, cw_ref, cb_ref, wdn_ref, gf_ref,
                       y_ref, convlast_ref, zbuf_ref, gbuf_ref, ybuf_ref, *, tile):
    s = pl.program_id(1)
    pad = SUBLANES
    q = tile // ROW_PHASES
    nslab_half = D_FF // LANES

    @pl.when(s == 0)
    def _reset_carry():
        zbuf_ref[:, 0:pad, :] = jnp.zeros((2 * nslab_half, pad, LANES), F32)

    x = x_ref[...]
    h = _rms(x, g_ref[...]).astype(BF16)
    for n0 in range(0, 2 * D_FF, FF_CHUNK):
        z = _dot(h, wup_ref[:, n0:n0 + FF_CHUNK])
        for jj in range(FF_CHUNK // LANES):
            zbuf_ref[n0 // LANES + jj, pad:pad + tile, :] = z[:, jj * LANES:(jj + 1) * LANES]

    def conv(slab, r):
        cs = slice(slab * LANES, (slab + 1) * LANES)
        acc = cb_ref[:, cs]
        for t in range(CONV_W):
            rows = pl.ds(pad + r - (CONV_W - 1 - t), q, stride=ROW_PHASES)
            acc = acc + cw_ref[t:t + 1, cs] * zbuf_ref[slab, rows, :]
        return acc

    for ja in range(nslab_half):
        for r in range(ROW_PHASES):
            gbuf_ref[r * q:(r + 1) * q, ja * LANES:(ja + 1) * LANES] = (
                _gelu(conv(ja, r)) * conv(ja + nslab_half, r)).astype(BF16)

    @pl.when(s == pl.num_programs(1) - 1)
    def _emit_state():
        for slab in range(2 * nslab_half):
            convlast_ref[:, slab * LANES:(slab + 1) * LANES] = (
                zbuf_ref[slab, pad + tile - (CONV_W - 1):pad + tile, :])

    zbuf_ref[:, 0:pad, :] = zbuf_ref[:, tile:tile + pad, :]
    yperm = _dot(gbuf_ref[...], wdn_ref[...])
    for j in range(D_MODEL // LANES):
        for r in range(ROW_PHASES):
            ybuf_ref[j, pl.ds(r, q, stride=ROW_PHASES), :] = yperm[r * q:(r + 1) * q, j * LANES:(j + 1) * LANES]
    y = jnp.concatenate([ybuf_ref[j] for j in range(D_MODEL // LANES)], axis=1)
    y_ref[...] = _rms(x + y, gf_ref[...])


def _prompt_ffn(x1, g_ffn, w_up, conv_w, conv_b, w_down, g_final):
    nb, seq, _ = x1.shape
    tile = PROMPT_TILE
    row_spec = pl.BlockSpec((None, tile, D_MODEL), lambda b, s: (b, s, 0))
    return pl.pallas_call(
        functools.partial(_prompt_ffn_kernel, tile=tile),
        grid=(nb, seq // tile),
        in_specs=[
            row_spec,
            _const_spec((1, D_MODEL)),
            _const_spec((D_MODEL, 2 * D_FF)),
            _const_spec((CONV_W, 2 * D_FF)),
            _const_spec((1, 2 * D_FF)),
            _const_spec((D_FF, D_MODEL)),
            _const_spec((1, D_MODEL)),
        ],
        out_specs=[row_spec, pl.BlockSpec((None, CONV_W - 1, 2 * D_FF), lambda b, s: (b, 0, 0))],
        out_shape=[
            jax.ShapeDtypeStruct((nb, seq, D_MODEL), F32),
            jax.ShapeDtypeStruct((nb, CONV_W - 1, 2 * D_FF), F32),
        ],
        scratch_shapes=[
            pltpu.VMEM((2 * D_FF // LANES, tile + SUBLANES, LANES), F32),
            pltpu.VMEM((tile, D_FF), BF16),
            pltpu.VMEM((D_MODEL // LANES, tile, LANES), F32),
        ],
        compiler_params=pltpu.CompilerParams(
            dimension_semantics=("arbitrary", "arbitrary"), vmem_limit_bytes=VMEM_LIMIT),
        name="prompt_ffn",
    )(x1, g_ffn, w_up, conv_w, conv_b, w_down, g_final)


def _sample_mixer_kernel(sinks_ref, x_ref, ck_ref, cv_ref, inv_ref, g_ref, win_ref, lng_ref, lnb_ref,
                         ws0_ref, bs0_ref, wpa_ref, wpb_ref, wo_ref,
                         x1_ref, gm_ref, ckn_ref, cvn_ref,
                         qz_ref, kn_ref, vn_ref, o_ref, *, chunk):
    lane = lax.broadcasted_iota(jnp.int32, (chunk, LANES), 1)
    low = lane < HEAD_DIM
    first_half = (lane % HEAD_DIM) < HEAD_DIM // 2

    x = x_ref[...]
    h = _rms(x, g_ref[...]).astype(BF16)
    a = _gelu(_dot(h, win_ref[:, 0:OFF_Q]))
    u = a[:, 0:GM_WIDTH]
    vg = _layer_norm(a[:, GM_WIDTH:OFF_Q], lng_ref[...], lnb_ref[...])
    gm_ref[...] = vg
    ya = (u * (vg * ws0_ref[...] + bs0_ref[...])).astype(BF16)

    zqkv = _dot(h, win_ref[:, OFF_Q:OFF_GA])
    pos = jnp.full((chunk, LANES), float(PAST_LEN), F32)
    cos, sin_signed = _rope_tables(pos, inv_ref[...], (chunk, LANES))
    knew = _rope(zqkv[:, Q_WIDTH:Q_WIDTH + KV_WIDTH], cos, sin_signed, first_half)
    vnew = zqkv[:, Q_WIDTH + KV_WIDTH:Q_WIDTH + 2 * KV_WIDTH]
    kn_ref[...] = knew
    vn_ref[...] = vnew
    for j in range(Q_WIDTH // LANES):
        qc = _rope(zqkv[:, j * LANES:(j + 1) * LANES], cos, sin_signed, first_half) * (HEAD_DIM ** -0.5)
        qsw = pltpu.roll(qc, HEAD_DIM, 1)
        g = j // 2
        even = jnp.where(low, qc, 0.0) if g == 0 else jnp.where(low, 0.0, qsw)
        odd = jnp.where(low, qsw, 0.0) if g == 0 else jnp.where(low, 0.0, qc)
        qz_ref[(2 * j) * chunk:(2 * j + 1) * chunk, :] = even
        qz_ref[(2 * j + 1) * chunk:(2 * j + 2) * chunk, :] = odd

    hrow = lax.broadcasted_iota(jnp.int32, (N_HEADS, 1), 0)
    sink_col = jnp.zeros((N_HEADS, 1), F32)
    for hd in range(N_HEADS):
        sink_col = jnp.where(hrow == hd, sinks_ref[hd], sink_col)
    head_low = lax.broadcasted_iota(jnp.int32, (N_HEADS, LANES), 0) < N_HEADS // KV_HEADS
    lane_low = lax.broadcasted_iota(jnp.int32, (N_HEADS, LANES), 1) < HEAD_DIM
    own_half = head_low == lane_low
    krow = lax.broadcasted_iota(jnp.int32, (CHUNK, LANES), 0)

    def body(bi, carry):
        qz = qz_ref[pl.ds(bi, N_HEADS, stride=chunk), :]
        kb = ck_ref[bi]
        vb = cv_ref[bi]
        kn = kn_ref[pl.ds(bi, 1), :]
        vn = vn_ref[pl.ds(bi, 1), :]
        sc = _dot_nt(qz.astype(BF16), kb.astype(BF16))
        sn = jnp.sum(qz * kn, axis=-1, keepdims=True)
        m = jnp.maximum(jnp.maximum(jnp.max(sc, axis=-1, keepdims=True), sn), sink_col)
        p = jnp.exp(sc - m)
        pn = jnp.exp(sn - m)
        lsum = jnp.sum(p, axis=-1, keepdims=True) + pn + jnp.exp(sink_col - m)
        o = (_dot(p.astype(BF16), vb.astype(BF16)) + pn * vn) / lsum
        o_ref[pl.ds(bi, N_HEADS, stride=chunk), :] = jnp.where(own_half, o, 0.0)
        ckn_ref[bi] = jnp.where(krow == CHUNK - 1, kn, pltpu.roll(kb, CHUNK - 1, 0))
        cvn_ref[bi] = jnp.where(krow == CHUNK - 1, vn, pltpu.roll(vb, CHUNK - 1, 0))
        return carry

    lax.fori_loop(0, chunk, body, 0)

    cols = []
    for j in range(Q_WIDTH // LANES):
        g = j // 2
        oe = o_ref[(2 * j) * chunk:(2 * j + 1) * chunk, :]
        oo = o_ref[(2 * j + 1) * chunk:(2 * j + 2) * chunk, :]
        if g == 0:
            cols.append(jnp.where(low, oe, pltpu.roll(oo, HEAD_DIM, 1)))
        else:
            cols.append(jnp.where(low, pltpu.roll(oe, HEAD_DIM, 1), oo))
    yb = jnp.concatenate(cols, axis=1).astype(BF16)

    gates = _dot(h, win_ref[:, OFF_GA:IN_WIDTH])
    merged = (_sigmoid(gates[:, 0:D_MODEL]) * _dot(ya, wpa_ref[...])
              + _sigmoid(gates[:, D_MODEL:2 * D_MODEL]) * _dot(yb, wpb_ref[...]))
    x1_ref[...] = x + _dot(merged.astype(BF16), wo_ref[...])


def _sample_mixer(x, ck, cv, sinks, inv_row, g_mix, w_in, ln_g, ln_b, ws0, bs0, w_pa, w_pb, w_o):
    n = x.shape[0]
    chunk = SAMPLE_CHUNK
    row = lambda w: pl.BlockSpec((chunk, w), lambda i: (i, 0))
    cache = pl.BlockSpec((chunk, CHUNK, KV_WIDTH), lambda i: (i, 0, 0))
    return pl.pallas_call(
        functools.partial(_sample_mixer_kernel, chunk=chunk),
        grid=(n // chunk,),
        in_specs=[
            pl.BlockSpec(memory_space=pltpu.SMEM),
            row(D_MODEL), cache, cache,
            _const_spec((1, LANES)),
            _const_spec((1, D_MODEL)),
            _const_spec((D_MODEL, IN_WIDTH)),
            _const_spec((1, GM_WIDTH)),
            _const_spec((1, GM_WIDTH)),
            _const_spec((1, GM_WIDTH)),
            _const_spec((1, GM_WIDTH)),
            _const_spec((GM_WIDTH, D_MODEL)),
            _const_spec((Q_WIDTH, D_MODEL)),
            _const_spec((D_MODEL, D_MODEL)),
        ],
        out_specs=[row(D_MODEL), row(GM_WIDTH), cache, cache],
        out_shape=[
            jax.ShapeDtypeStruct((n, D_MODEL), F32),
            jax.ShapeDtypeStruct((n, GM_WIDTH), F32),
            jax.ShapeDtypeStruct((n, CHUNK, KV_WIDTH), F32),
            jax.ShapeDtypeStruct((n, CHUNK, KV_WIDTH), F32),
        ],
        scratch_shapes=[
            pltpu.VMEM((N_HEADS * chunk, LANES), F32),
            pltpu.VMEM((chunk, KV_WIDTH), F32),
            pltpu.VMEM((chunk, KV_WIDTH), F32),
            pltpu.VMEM((N_HEADS * chunk, LANES), F32),
        ],
        compiler_params=pltpu.CompilerParams(
            dimension_semantics=("arbitrary",), vmem_limit_bytes=VMEM_LIMIT),
        name="sample_mixer",
    )(sinks, x, ck, cv, inv_row, g_mix, w_in, ln_g, ln_b, ws0, bs0, w_pa, w_pb, w_o)


def _sample_ffn_kernel(x_ref, g_ref, wua_ref, wub_ref, cwa_ref, cwb_ref, cba_ref, cbb_ref,
                       p0a_ref, p0b_ref, p1a_ref, p1b_ref, wdn_ref, gf_ref,
                       y_ref, za_ref, zb_ref, acc_ref):
    j = pl.program_id(0)
    x = x_ref[...]
    h = _rms(x, g_ref[...]).astype(BF16)
    za = _dot(h, wua_ref[...])
    zb = _dot(h, wub_ref[...])
    za_ref[...] = za
    zb_ref[...] = zb
    ca = cba_ref[...] + cwa_ref[0:1, :] * p0a_ref[...] + cwa_ref[1:2, :] * p1a_ref[...] + cwa_ref[2:3, :] * za
    cb = cbb_ref[...] + cwb_ref[0:1, :] * p0b_ref[...] + cwb_ref[1:2, :] * p1b_ref[...] + cwb_ref[2:3, :] * zb
    part = _dot((_gelu(ca) * cb).astype(BF16), wdn_ref[...])

    @pl.when(j == 0)
    def _first():
        acc_ref[...] = part

    @pl.when(j > 0)
    def _rest():
        acc_ref[...] += part

    @pl.when(j == pl.num_programs(0) - 1)
    def _finish():
        y_ref[...] = _rms(x + acc_ref[...], gf_ref[...])


def _sample_ffn(x1, state2d, g_ffn, w_up, conv_w, conv_b, w_down, g_final):
    n = x1.shape[0]
    nchunk = D_FF // FF_CHUNK
    full = lambda w: pl.BlockSpec((n, w), lambda j: (0, 0))
    col = lambda rows, off: pl.BlockSpec((rows, FF_CHUNK), lambda j, off=off: (0, j + off))
    return pl.pallas_call(
        _sample_ffn_kernel,
        grid=(nchunk,),
        in_specs=[
            full(D_MODEL),
            pl.BlockSpec((1, D_MODEL), lambda j: (0, 0)),
            col(D_MODEL, 0), col(D_MODEL, nchunk),
            col(CONV_W, 0), col(CONV_W, nchunk),
            col(1, 0), col(1, nchunk),
            col(n, 0), col(n, nchunk), col(n, 2 * nchunk), col(n, 3 * nchunk),
            pl.BlockSpec((FF_CHUNK, D_MODEL), lambda j: (j, 0)),
            pl.BlockSpec((1, D_MODEL), lambda j: (0, 0)),
        ],
        out_specs=[full(D_MODEL), col(n, 0), col(n, 0)],
        out_shape=[
            jax.ShapeDtypeStruct((n, D_MODEL), F32),
            jax.ShapeDtypeStruct((n, D_FF), F32),
            jax.ShapeDtypeStruct((n, D_FF), F32),
        ],
        scratch_shapes=[pltpu.VMEM((n, D_MODEL), F32)],
        compiler_params=pltpu.CompilerParams(
            dimension_semantics=("arbitrary",), vmem_limit_bytes=VMEM_LIMIT),
        name="sample_ffn",
    )(x1, g_ffn, w_up, w_up, conv_w, conv_w, conv_b, conv_b,
      state2d, state2d, state2d, state2d, w_down, g_final)


def kernel(x_prompt, x_sample, cache_swa_k, cache_swa_v, state_ffn_conv, g_mix, w_in, ln_v_g, ln_v_b,
           w_s, b_s, sinks, w_pa, w_pb, w_o, g_ffn, w_up, conv_w, conv_b, w_down, g_final):
    depth = g_mix.shape[0]
    assert depth == 1
    nb = x_prompt.shape[0]
    nd = x_sample.shape[0]
    half = HEAD_DIM // 2
    inv = ROPE_THETA ** (-jnp.arange(half, dtype=F32) / half)
    inv_row = jnp.tile(inv, LANES // half)[None, :]

    l = 0
    row = lambda v: v[l][None, :]
    w_in_b = w_in[l].astype(BF16)
    w_pa_b = w_pa[l].astype(BF16)
    w_pb_b = w_pb[l].astype(BF16)
    w_o_b = w_o[l].astype(BF16)
    w_up_b = w_up[l].astype(BF16)
    w_dn_b = w_down[l].astype(BF16)
    bs_full = jnp.repeat(b_s[l].T, HEAD_DIM, axis=1)
    ws0 = jnp.repeat(w_s[l][:, 0, 0], HEAD_DIM)[None, :]
    bs0 = bs_full[0:1]

    x1p, kp, vp, gmp = _prompt_mixer(x_prompt, sinks[l], inv_row, row(g_mix), w_in_b, row(ln_v_g), row(ln_v_b),
                                     w_s[l], bs_full, w_pa_b, w_pb_b, w_o_b)
    yp, convp = _prompt_ffn(x1p, row(g_ffn), w_up_b, conv_w[l], row(conv_b), w_dn_b, g_final[None, :])

    xs = x_sample.reshape(nd, D_MODEL)
    ck = cache_swa_k[l].reshape(nd, CHUNK, KV_WIDTH)
    cv = cache_swa_v[l].reshape(nd, CHUNK, KV_WIDTH)
    x1s, gms, ckn, cvn = _sample_mixer(xs, ck, cv, sinks[l], inv_row, row(g_mix), w_in_b, row(ln_v_g),
                                       row(ln_v_b), ws0, bs0, w_pa_b, w_pb_b, w_o_b)
    state = state_ffn_conv[l]
    ys, za, zb = _sample_ffn(x1s, state.reshape(nd, (CONV_W - 1) * 2 * D_FF), row(g_ffn), w_up_b,
                             conv_w[l], row(conv_b), w_dn_b, g_final[None, :])
    conv_s = jnp.stack([state[:, 1, :], jnp.concatenate([za, zb], axis=1)], axis=1)

    return (yp,
            ys.reshape(nd, 1, D_MODEL),
            kp.reshape(1, nb, CHUNK, KV_HEADS, HEAD_DIM),
            vp.reshape(1, nb, CHUNK, KV_HEADS, HEAD_DIM),
            gmp.reshape(1, nb, CHUNK, GM_HEADS, GM_WIDTH // GM_HEADS),
            convp[None],
            ckn.reshape(1, nd, CHUNK, KV_HEADS, HEAD_DIM),
            cvn.reshape(1, nd, CHUNK, KV_HEADS, HEAD_DIM),
            gms.reshape(1, nd, 1, GM_HEADS, GM_WIDTH // GM_HEADS),
            conv_s[None])
```

```python
import functools
import math

import jax
import jax.numpy as jnp
from jax import lax
from jax.experimental import pallas as pl
from jax.experimental.pallas import tpu as pltpu

F32 = jnp.float32
BF16 = jnp.bfloat16

D_MODEL = 1024
SEQ = 2048
PAST_LEN = 16384
CHUNK = 128
GM_HEADS = 8
GM_WIDTH = 512
N_HEADS = 8
KV_HEADS = 2
HEAD_DIM = 64
Q_WIDTH = N_HEADS * HEAD_DIM
KV_WIDTH = KV_HEADS * HEAD_DIM
D_FF = 3 * D_MODEL
CONV_W = 3
EPS = 1e-6
ROPE_THETA = 10000.0
LANES = 128
SUBLANES = 8
OFF_Q = 2 * GM_WIDTH
OFF_K = OFF_Q + Q_WIDTH
OFF_V = OFF_K + KV_WIDTH
OFF_GA = OFF_V + KV_WIDTH
OFF_GB = OFF_GA + D_MODEL
IN_WIDTH = OFF_GB + D_MODEL

PROMPT_TILE = 256
SAMPLE_CHUNK = 16
SAMPLE_UNROLL = 8
FF_CHUNK = 512
ROW_PHASES = 4
VMEM_LIMIT = 52 * 1024 * 1024

_GELU_C0 = math.sqrt(2.0 / math.pi)
_GELU_C1 = _GELU_C0 * 0.044715


def _gelu(x):
    inner = x * (_GELU_C0 + _GELU_C1 * (x * x))
    hx = 0.5 * x
    return hx + hx * jnp.tanh(inner)


def _rms(x, g):
    ms = jnp.mean(x * x, axis=-1, keepdims=True)
    return x * lax.rsqrt(ms + EPS) * g


def _layer_norm(x, g, b):
    mu = jnp.mean(x, axis=-1, keepdims=True)
    xc = x - mu
    var = jnp.mean(xc * xc, axis=-1, keepdims=True)
    return xc * lax.rsqrt(var + EPS) * g + b


def _sigmoid(x):
    return 1.0 / (1.0 + jnp.exp(-x))


def _dot(a, b):
    return jnp.dot(a, b, preferred_element_type=F32)


def _dot_nt(a, b):
    return lax.dot_general(a, b, (((1,), (1,)), ((), ())), preferred_element_type=F32)


def _rope(xc, cos, sin_signed, first_half):
    partner = jnp.where(first_half, pltpu.roll(xc, LANES - 32, 1), pltpu.roll(xc, 32, 1))
    return xc * cos + partner * sin_signed


def _rope_tables(pos_f32, inv_row, shape):
    lane = lax.broadcasted_iota(jnp.int32, shape, 1)
    ang = pos_f32 * inv_row
    sin = jnp.sin(ang)
    return jnp.cos(ang), jnp.where((lane % HEAD_DIM) < HEAD_DIM // 2, -sin, sin)


def _prompt_mixer_kernel(sinks_ref, x_ref, inv_ref, g_ref, win_ref, lng_ref, lnb_ref, ws_ref, bs_ref,
                         wpa_ref, wpb_ref, wo_ref,
                         x1_ref, klast_ref, vlast_ref, gmlast_ref,
                         cos_ref, sin_ref, wcat_ref, kprev_ref, vprev_ref, h_ref, *, tile):
    b = pl.program_id(0)
    s = pl.program_id(1)
    nblk = tile // CHUNK

    @pl.when((b == 0) & (s == 0))
    def _init_tables():
        def body(i, carry):
            r0 = pl.multiple_of(i * CHUNK, CHUNK)
            pos = (lax.broadcasted_iota(jnp.int32, (CHUNK, LANES), 0) + r0).astype(F32)
            cos, sin_signed = _rope_tables(pos, inv_ref[...], (CHUNK, LANES))
            cos_ref[pl.ds(r0, CHUNK), :] = cos
            sin_ref[pl.ds(r0, CHUNK), :] = sin_signed
            return carry
        lax.fori_loop(0, SEQ // CHUNK, body, 0)
        row = lax.broadcasted_iota(jnp.int32, (CHUNK, CHUNK), 0)
        col = lax.broadcasted_iota(jnp.int32, (CHUNK, CHUNK), 1)
        causal = col <= row
        for j in range(GM_HEADS // 2):
            wcat_ref[j, :, 0:CHUNK] = jnp.where(causal, ws_ref[2 * j], 0.0).astype(BF16)
            wcat_ref[j, :, CHUNK:2 * CHUNK] = jnp.where(causal, ws_ref[2 * j + 1], 0.0).astype(BF16)

    @pl.when(s == 0)
    def _reset_carry():
        kprev_ref[...] = jnp.zeros_like(kprev_ref)
        vprev_ref[...] = jnp.zeros_like(vprev_ref)

    lane = lax.broadcasted_iota(jnp.int32, (CHUNK, LANES), 1)
    low = lane < HEAD_DIM
    low2 = lax.broadcasted_iota(jnp.int32, (2 * CHUNK, LANES), 1) < HEAD_DIM
    first_half_t = (lax.broadcasted_iota(jnp.int32, (tile, LANES), 1) % HEAD_DIM) < HEAD_DIM // 2
    row = lax.broadcasted_iota(jnp.int32, (CHUNK, CHUNK), 0)
    col = lax.broadcasted_iota(jnp.int32, (CHUNK, CHUNK), 1)
    cur_mask = col <= row
    prev_live = (col - row) >= jnp.where(s > 0, 0, 2 * CHUNK)
    prev_inner = col >= row

    x = x_ref[...]
    h_ref[...] = _rms(x, g_ref[...]).astype(BF16)

    zqkv = _dot(h_ref[...], win_ref[:, OFF_Q:OFF_GA])
    zuv = _dot(h_ref[...], win_ref[:, 0:OFF_Q])

    r0 = pl.multiple_of(s * tile, tile)
    cos = cos_ref[pl.ds(r0, tile), :]
    sin_signed = sin_ref[pl.ds(r0, tile), :]
    qcols = [(_rope(zqkv[:, j * LANES:(j + 1) * LANES], cos, sin_signed, first_half_t)
              * (HEAD_DIM ** -0.5)).astype(BF16) for j in range(Q_WIDTH // LANES)]
    krot = _rope(zqkv[:, Q_WIDTH:Q_WIDTH + KV_WIDTH], cos, sin_signed, first_half_t)
    vval = zqkv[:, Q_WIDTH + KV_WIDTH:Q_WIDTH + 2 * KV_WIDTH]

    kp = kprev_ref[...]
    vp = vprev_ref[...]
    scores = {}
    vblks = {}
    for i in range(nblk):
        r = slice(i * CHUNK, (i + 1) * CHUNK)
        kk = jnp.concatenate([kp, krot[r]], axis=0)
        vv = jnp.concatenate([vp, vval[r]], axis=0)
        kk_sw = pltpu.roll(kk, HEAD_DIM, 1)
        vv_sw = pltpu.roll(vv, HEAD_DIM, 1)
        for g in range(KV_HEADS):
            k_lo = jnp.where(low2, kk if g == 0 else kk_sw, 0.0).astype(BF16)
            k_hi = jnp.where(low2, 0.0, kk_sw if g == 0 else kk).astype(BF16)
            v_lo = jnp.where(low2, vv if g == 0 else vv_sw, 0.0).astype(BF16)
            v_hi = jnp.where(low2, 0.0, vv_sw if g == 0 else vv).astype(BF16)
            krhs = jnp.concatenate([k_lo, k_hi], axis=0)
            vblks[i, g] = jnp.concatenate([v_lo, v_hi], axis=0)
            qstack = jnp.concatenate([qcols[2 * g][r], qcols[2 * g + 1][r]], axis=0)
            scores[i, g] = _dot_nt(qstack, krhs)
        kp = krot[r]
        vp = vval[r]
    kprev_ref[...] = kp
    vprev_ref[...] = vp

    gates = _dot(h_ref[...], win_ref[:, OFF_GA:IN_WIDTH])

    a = _gelu(zuv)
    u = a[:, 0:GM_WIDTH]
    vg = _layer_norm(a[:, GM_WIDTH:OFF_Q], lng_ref[...], lnb_ref[...])
    ya_blocks = []
    for c in range(nblk):
        r = slice(c * CHUNK, (c + 1) * CHUNK)
        cols = []
        for j in range(GM_HEADS // 2):
            vcol = vg[r, j * LANES:(j + 1) * LANES]
            rhs = jnp.concatenate([jnp.where(low, vcol, 0.0), jnp.where(low, 0.0, vcol)], axis=0).astype(BF16)
            cols.append(_dot(wcat_ref[j], rhs))
        mix = jnp.concatenate(cols, axis=1) + bs_ref[...]
        ya_blocks.append((u[r] * mix).astype(BF16))
    ya = jnp.concatenate(ya_blocks, axis=0)
    pa = _dot(ya, wpa_ref[...])

    yb_blocks = []
    for i in range(nblk):
        pmask = prev_live if i == 0 else prev_inner
        cols_out = []
        for g in range(KV_HEADS):
            sc = scores[i, g]
            p_rows = []
            rinv_rows = []
            for jj in range(2):
                ps = []
                ls = []
                for par in range(2):
                    hd = 4 * g + 2 * jj + par
                    sh = sc[jj * CHUNK:(jj + 1) * CHUNK, par * 2 * CHUNK:(par + 1) * 2 * CHUNK]
                    sp = jnp.where(pmask, sh[:, 0:CHUNK], -jnp.inf)
                    scur = jnp.where(cur_mask, sh[:, CHUNK:2 * CHUNK], -jnp.inf)
                    sink = sinks_ref[hd]
                    m = jnp.maximum(jnp.max(jnp.maximum(sp, scur), axis=-1, keepdims=True), sink)
                    pp = jnp.exp(sp - m)
                    pc = jnp.exp(scur - m)
                    lsum = jnp.sum(pp + pc, axis=-1, keepdims=True) + jnp.exp(sink - m)
                    ps.append(pp.astype(BF16))
                    ps.append(pc.astype(BF16))
                    ls.append(lsum)
                p_rows.append(jnp.concatenate(ps, axis=1))
                rinv_rows.append(jnp.where(low, 1.0 / ls[0], 1.0 / ls[1]))
            o = _dot(jnp.concatenate(p_rows, axis=0), vblks[i, g])
            cols_out.append(o[0:CHUNK] * rinv_rows[0])
            cols_out.append(o[CHUNK:2 * CHUNK] * rinv_rows[1])
        yb_blocks.append(jnp.concatenate(cols_out, axis=1).astype(BF16))
    yb = jnp.concatenate(yb_blocks, axis=0)

    merged = (_sigmoid(gates[:, 0:D_MODEL]) * pa
              + _sigmoid(gates[:, D_MODEL:2 * D_MODEL]) * _dot(yb, wpb_ref[...]))
    x1_ref[...] = x + _dot(merged.astype(BF16), wo_ref[...])

    @pl.when(s == pl.num_programs(1) - 1)
    def _emit_state():
        klast_ref[...] = krot[tile - CHUNK:tile]
        vlast_ref[...] = vval[tile - CHUNK:tile]
        gmlast_ref[...] = vg[tile - CHUNK:tile]


def _const_spec(shape):
    nd = len(shape)
    return pl.BlockSpec(shape, lambda *_: (0,) * nd, pipeline_mode=pl.Buffered(1))


def _prompt_mixer(x, sinks, inv_row, g_mix, w_in, ln_g, ln_b, w_s, bs_full, w_pa, w_pb, w_o):
    nb, seq, _ = x.shape
    tile = PROMPT_TILE
    grid = (nb, seq // tile)
    row_spec = pl.BlockSpec((None, tile, D_MODEL), lambda b, s: (b, s, 0))
    last = lambda w: pl.BlockSpec((None, CHUNK, w), lambda b, s: (b, 0, 0))
    return pl.pallas_call(
        functools.partial(_prompt_mixer_kernel, tile=tile),
        grid=grid,
        in_specs=[
            pl.BlockSpec(memory_space=pltpu.SMEM),
            row_spec,
            _const_spec((1, LANES)),
            _const_spec((1, D_MODEL)),
            _const_spec((D_MODEL, IN_WIDTH)),
            _const_spec((1, GM_WIDTH)),
            _const_spec((1, GM_WIDTH)),
            _const_spec((GM_HEADS, CHUNK, CHUNK)),
            _const_spec((CHUNK, GM_WIDTH)),
            _const_spec((GM_WIDTH, D_MODEL)),
            _const_spec((Q_WIDTH, D_MODEL)),
            _const_spec((D_MODEL, D_MODEL)),
        ],
        out_specs=[row_spec, last(KV_WIDTH), last(KV_WIDTH), last(GM_WIDTH)],
        out_shape=[
            jax.ShapeDtypeStruct((nb, seq, D_MODEL), F32),
            jax.ShapeDtypeStruct((nb, CHUNK, KV_WIDTH), F32),
            jax.ShapeDtypeStruct((nb, CHUNK, KV_WIDTH), F32),
            jax.ShapeDtypeStruct((nb, CHUNK, GM_WIDTH), F32),
        ],
        scratch_shapes=[
            pltpu.VMEM((SEQ, LANES), F32),
            pltpu.VMEM((SEQ, LANES), F32),
            pltpu.VMEM((GM_HEADS // 2, CHUNK, 2 * CHUNK), BF16),
            pltpu.VMEM((CHUNK, KV_WIDTH), F32),
            pltpu.VMEM((CHUNK, KV_WIDTH), F32),
            pltpu.VMEM((tile, D_MODEL), BF16),
        ],
        compiler_params=pltpu.CompilerParams(
            dimension_semantics=("arbitrary", "arbitrary"), vmem_limit_bytes=VMEM_LIMIT),
        name="prompt_mixer",
    )(sinks, x, inv_row, g_mix, w_in, ln_g, ln_b, w_s, bs_full, w_pa, w_pb, w_o)


def _prompt_ffn_kernel(x_ref, g_ref, wup_ref, cw_ref, cb_ref, wdn_ref, gf_ref,
                       y_ref, convlast_ref, zbuf_ref, gbuf_ref, ybuf_ref, h_ref, *, tile):
    s = pl.program_id(1)
    pad = SUBLANES
    q = tile // ROW_PHASES
    nslab_half = D_FF // LANES

    @pl.when(s == 0)
    def _reset_carry():
        zbuf_ref[:, 0:pad, :] = jnp.zeros((2 * nslab_half, pad, LANES), F32)

    x = x_ref[...]
    h_ref[...] = _rms(x, g_ref[...]).astype(BF16)

    def up_project(j):
        for half in range(2):
            n0 = half * D_FF + j * FF_CHUNK
            z = _dot(h_ref[...], wup_ref[:, n0:n0 + FF_CHUNK])
            for jj in range(FF_CHUNK // LANES):
                zbuf_ref[n0 // LANES + jj, pad:pad + tile, :] = z[:, jj * LANES:(jj + 1) * LANES]

    def conv(slab, r):
        cs = slice(slab * LANES, (slab + 1) * LANES)
        acc = cb_ref[:, cs]
        for t in range(CONV_W):
            rows = pl.ds(pad + r - (CONV_W - 1 - t), q, stride=ROW_PHASES)
            acc = acc + cw_ref[t:t + 1, cs] * zbuf_ref[slab, rows, :]
        return acc

    def gate_and_project(j, yperm):
        for ja in range(j * (FF_CHUNK // LANES), (j + 1) * (FF_CHUNK // LANES)):
            for r in range(ROW_PHASES):
                gbuf_ref[r * q:(r + 1) * q, ja * LANES:(ja + 1) * LANES] = (
                    _gelu(conv(ja, r)) * conv(ja + nslab_half, r)).astype(BF16)
        cs = slice(j * FF_CHUNK, (j + 1) * FF_CHUNK)
        part = _dot(gbuf_ref[:, cs], wdn_ref[cs, :])
        return part if yperm is None else yperm + part

    nchunk = D_FF // FF_CHUNK
    yperm = None
    up_project(0)
    for j in range(1, nchunk):
        up_project(j)
        yperm = gate_and_project(j - 1, yperm)
    yperm = gate_and_project(nchunk - 1, yperm)

    for j in range(D_MODEL // LANES):
        for r in range(ROW_PHASES):
            ybuf_ref[j, pl.ds(r, q, stride=ROW_PHASES), :] = yperm[r * q:(r + 1) * q, j * LANES:(j + 1) * LANES]
    y = jnp.concatenate([ybuf_ref[j] for j in range(D_MODEL // LANES)], axis=1)
    y_ref[...] = _rms(x + y, gf_ref[...])

    @pl.when(s == pl.num_programs(1) - 1)
    def _emit_state():
        for slab in range(2 * nslab_half):
            convlast_ref[:, slab * LANES:(slab + 1) * LANES] = (
                zbuf_ref[slab, pad + tile - (CONV_W - 1):pad + tile, :])

    zbuf_ref[:, 0:pad, :] = zbuf_ref[:, tile:tile + pad, :]


def _prompt_ffn(x1, g_ffn, w_up, conv_w, conv_b, w_down, g_final):
    nb, seq, _ = x1.shape
    tile = PROMPT_TILE
    row_spec = pl.BlockSpec((None, tile, D_MODEL), lambda b, s: (b, s, 0))
    return pl.pallas_call(
        functools.partial(_prompt_ffn_kernel, tile=tile),
        grid=(nb, seq // tile),
        in_specs=[
            row_spec,
            _const_spec((1, D_MODEL)),
            _const_spec((D_MODEL, 2 * D_FF)),
            _const_spec((CONV_W, 2 * D_FF)),
            _const_spec((1, 2 * D_FF)),
            _const_spec((D_FF, D_MODEL)),
            _const_spec((1, D_MODEL)),
        ],
        out_specs=[row_spec, pl.BlockSpec((None, CONV_W - 1, 2 * D_FF), lambda b, s: (b, 0, 0))],
        out_shape=[
            jax.ShapeDtypeStruct((nb, seq, D_MODEL), F32),
            jax.ShapeDtypeStruct((nb, CONV_W - 1, 2 * D_FF), F32),
        ],
        scratch_shapes=[
            pltpu.VMEM((2 * D_FF // LANES, tile + SUBLANES, LANES), F32),
            pltpu.VMEM((tile, D_FF), BF16),
            pltpu.VMEM((D_MODEL // LANES, tile, LANES), F32),
            pltpu.VMEM((tile, D_MODEL), BF16),
        ],
        compiler_params=pltpu.CompilerParams(
            dimension_semantics=("arbitrary", "arbitrary"), vmem_limit_bytes=VMEM_LIMIT),
        name="prompt_ffn",
    )(x1, g_ffn, w_up, conv_w, conv_b, w_down, g_final)


def _sample_mixer_kernel(sinks_ref, x_ref, ck_ref, cv_ref, inv_ref, g_ref, win_ref, lng_ref, lnb_ref,
                         ws0_ref, bs0_ref, wpa_ref, wpb_ref, wo_ref,
                         x1_ref, gm_ref, ckn_ref, cvn_ref,
                         qz_ref, kn_ref, vn_ref, o_ref, *, chunk):
    lane = lax.broadcasted_iota(jnp.int32, (chunk, LANES), 1)
    low = lane < HEAD_DIM
    first_half = (lane % HEAD_DIM) < HEAD_DIM // 2

    x = x_ref[...]
    h = _rms(x, g_ref[...]).astype(BF16)
    a = _gelu(_dot(h, win_ref[:, 0:OFF_Q]))
    u = a[:, 0:GM_WIDTH]
    vg = _layer_norm(a[:, GM_WIDTH:OFF_Q], lng_ref[...], lnb_ref[...])
    gm_ref[...] = vg
    ya = (u * (vg * ws0_ref[...] + bs0_ref[...])).astype(BF16)

    zqkv = _dot(h, win_ref[:, OFF_Q:OFF_GA])
    pos = jnp.full((chunk, LANES), float(PAST_LEN), F32)
    cos, sin_signed = _rope_tables(pos, inv_ref[...], (chunk, LANES))
    knew = _rope(zqkv[:, Q_WIDTH:Q_WIDTH + KV_WIDTH], cos, sin_signed, first_half)
    vnew = zqkv[:, Q_WIDTH + KV_WIDTH:Q_WIDTH + 2 * KV_WIDTH]
    kn_ref[...] = knew
    vn_ref[...] = vnew
    for j in range(Q_WIDTH // LANES):
        qc = _rope(zqkv[:, j * LANES:(j + 1) * LANES], cos, sin_signed, first_half) * (HEAD_DIM ** -0.5)
        qsw = pltpu.roll(qc, HEAD_DIM, 1)
        g = j // 2
        even = jnp.where(low, qc, 0.0) if g == 0 else jnp.where(low, 0.0, qsw)
        odd = jnp.where(low, qsw, 0.0) if g == 0 else jnp.where(low, 0.0, qc)
        qz_ref[(2 * j) * chunk:(2 * j + 1) * chunk, :] = even
        qz_ref[(2 * j + 1) * chunk:(2 * j + 2) * chunk, :] = odd

    hrow = lax.broadcasted_iota(jnp.int32, (N_HEADS, 1), 0)
    sink_col = jnp.zeros((N_HEADS, 1), F32)
    for hd in range(N_HEADS):
        sink_col = jnp.where(hrow == hd, sinks_ref[hd], sink_col)
    head_low = lax.broadcasted_iota(jnp.int32, (N_HEADS, LANES), 0) < N_HEADS // KV_HEADS
    lane_low = lax.broadcasted_iota(jnp.int32, (N_HEADS, LANES), 1) < HEAD_DIM
    own_half = head_low == lane_low
    krow = lax.broadcasted_iota(jnp.int32, (CHUNK, LANES), 0)

    def body(bi, carry):
        qz = qz_ref[pl.ds(bi, N_HEADS, stride=chunk), :]
        kb = ck_ref[bi]
        vb = cv_ref[bi]
        kn = kn_ref[pl.ds(bi, 1), :]
        vn = vn_ref[pl.ds(bi, 1), :]
        sc = _dot_nt(qz.astype(BF16), kb.astype(BF16))
        sn = jnp.sum(qz * kn, axis=-1, keepdims=True)
        m = jnp.maximum(jnp.maximum(jnp.max(sc, axis=-1, keepdims=True), sn), sink_col)
        p = jnp.exp(sc - m)
        pn = jnp.exp(sn - m)
        lsum = jnp.sum(p, axis=-1, keepdims=True) + pn + jnp.exp(sink_col - m)
        o = (_dot(p.astype(BF16), vb.astype(BF16)) + pn * vn) / lsum
        o_ref[pl.ds(bi, N_HEADS, stride=chunk), :] = jnp.where(own_half, o, 0.0)
        ckn_ref[bi] = jnp.where(krow == CHUNK - 1, kn, pltpu.roll(kb, CHUNK - 1, 0))
        cvn_ref[bi] = jnp.where(krow == CHUNK - 1, vn, pltpu.roll(vb, CHUNK - 1, 0))
        return carry

    lax.fori_loop(0, chunk, body, 0, unroll=SAMPLE_UNROLL)

    cols = []
    for j in range(Q_WIDTH // LANES):
        g = j // 2
        oe = o_ref[(2 * j) * chunk:(2 * j + 1) * chunk, :]
        oo = o_ref[(2 * j + 1) * chunk:(2 * j + 2) * chunk, :]
        if g == 0:
            cols.append(jnp.where(low, oe, pltpu.roll(oo, HEAD_DIM, 1)))
        else:
            cols.append(jnp.where(low, pltpu.roll(oe, HEAD_DIM, 1), oo))
    yb = jnp.concatenate(cols, axis=1).astype(BF16)

    gates = _dot(h, win_ref[:, OFF_GA:IN_WIDTH])
    merged = (_sigmoid(gates[:, 0:D_MODEL]) * _dot(ya, wpa_ref[...])
              + _sigmoid(gates[:, D_MODEL:2 * D_MODEL]) * _dot(yb, wpb_ref[...]))
    x1_ref[...] = x + _dot(merged.astype(BF16), wo_ref[...])


def _sample_mixer(x, ck, cv, sinks, inv_row, g_mix, w_in, ln_g, ln_b, ws0, bs0, w_pa, w_pb, w_o):
    n = x.shape[0]
    chunk = SAMPLE_CHUNK
    row = lambda w: pl.BlockSpec((chunk, w), lambda i: (i, 0))
    cache = pl.BlockSpec((chunk, CHUNK, KV_WIDTH), lambda i: (i, 0, 0))
    return pl.pallas_call(
        functools.partial(_sample_mixer_kernel, chunk=chunk),
        grid=(n // chunk,),
        in_specs=[
            pl.BlockSpec(memory_space=pltpu.SMEM),
            row(D_MODEL), cache, cache,
            _const_spec((1, LANES)),
            _const_spec((1, D_MODEL)),
            _const_spec((D_MODEL, IN_WIDTH)),
            _const_spec((1, GM_WIDTH)),
            _const_spec((1, GM_WIDTH)),
            _const_spec((1, GM_WIDTH)),
            _const_spec((1, GM_WIDTH)),
            _const_spec((GM_WIDTH, D_MODEL)),
            _const_spec((Q_WIDTH, D_MODEL)),
            _const_spec((D_MODEL, D_MODEL)),
        ],
        out_specs=[row(D_MODEL), row(GM_WIDTH), cache, cache],
        out_shape=[
            jax.ShapeDtypeStruct((n, D_MODEL), F32),
            jax.ShapeDtypeStruct((n, GM_WIDTH), F32),
            jax.ShapeDtypeStruct((n, CHUNK, KV_WIDTH), F32),
            jax.ShapeDtypeStruct((n, CHUNK, KV_WIDTH), F32),
        ],
        scratch_shapes=[
            pltpu.VMEM((N_HEADS * chunk, LANES), F32),
            pltpu.VMEM((chunk, KV_WIDTH), F32),
            pltpu.VMEM((chunk, KV_WIDTH), F32),
            pltpu.VMEM((N_HEADS * chunk, LANES), F32),
        ],
        compiler_params=pltpu.CompilerParams(
            dimension_semantics=("arbitrary",), vmem_limit_bytes=VMEM_LIMIT),
        name="sample_mixer",
    )(sinks, x, ck, cv, inv_row, g_mix, w_in, ln_g, ln_b, ws0, bs0, w_pa, w_pb, w_o)


def _sample_ffn_kernel(x_ref, g_ref, wua_ref, wub_ref, cwa_ref, cwb_ref, cba_ref, cbb_ref,
                       p0a_ref, p0b_ref, p1a_ref, p1b_ref, wdn_ref, gf_ref,
                       y_ref, za_ref, zb_ref, acc_ref):
    j = pl.program_id(0)
    x = x_ref[...]
    h = _rms(x, g_ref[...]).astype(BF16)
    za = _dot(h, wua_ref[...])
    zb = _dot(h, wub_ref[...])
    za_ref[...] = za
    zb_ref[...] = zb
    ca = cba_ref[...] + cwa_ref[0:1, :] * p0a_ref[...] + cwa_ref[1:2, :] * p1a_ref[...] + cwa_ref[2:3, :] * za
    cb = cbb_ref[...] + cwb_ref[0:1, :] * p0b_ref[...] + cwb_ref[1:2, :] * p1b_ref[...] + cwb_ref[2:3, :] * zb
    part = _dot((_gelu(ca) * cb).astype(BF16), wdn_ref[...])

    @pl.when(j == 0)
    def _first():
        acc_ref[...] = part

    @pl.when(j > 0)
    def _rest():
        acc_ref[...] += part

    @pl.when(j == pl.num_programs(0) - 1)
    def _finish():
        y_ref[...] = _rms(x + acc_ref[...], gf_ref[...])


def _sample_ffn(x1, state2d, g_ffn, w_up, conv_w, conv_b, w_down, g_final):
    n = x1.shape[0]
    nchunk = D_FF // FF_CHUNK
    full = lambda w: pl.BlockSpec((n, w), lambda j: (0, 0))
    col = lambda rows, off: pl.BlockSpec((rows, FF_CHUNK), lambda j, off=off: (0, j + off))
    return pl.pallas_call(
        _sample_ffn_kernel,
        grid=(nchunk,),
        in_specs=[
            full(D_MODEL),
            pl.BlockSpec((1, D_MODEL), lambda j: (0, 0)),
            col(D_MODEL, 0), col(D_MODEL, nchunk),
            col(CONV_W, 0), col(CONV_W, nchunk),
            col(1, 0), col(1, nchunk),
            col(n, 0), col(n, nchunk), col(n, 2 * nchunk), col(n, 3 * nchunk),
            pl.BlockSpec((FF_CHUNK, D_MODEL), lambda j: (j, 0)),
            pl.BlockSpec((1, D_MODEL), lambda j: (0, 0)),
        ],
        out_specs=[full(D_MODEL), col(n, 0), col(n, 0)],
        out_shape=[
            jax.ShapeDtypeStruct((n, D_MODEL), F32),
            jax.ShapeDtypeStruct((n, D_FF), F32),
            jax.ShapeDtypeStruct((n, D_FF), F32),
        ],
        scratch_shapes=[pltpu.VMEM((n, D_MODEL), F32)],
        compiler_params=pltpu.CompilerParams(
            dimension_semantics=("arbitrary",), vmem_limit_bytes=VMEM_LIMIT),
        name="sample_ffn",
    )(x1, g_ffn, w_up, w_up, conv_w, conv_w, conv_b, conv_b,
      state2d, state2d, state2d, state2d, w_down, g_final)


def kernel(x_prompt, x_sample, cache_swa_k, cache_swa_v, state_ffn_conv, g_mix, w_in, ln_v_g, ln_v_b,
           w_s, b_s, sinks, w_pa, w_pb, w_o, g_ffn, w_up, conv_w, conv_b, w_down, g_final):
    depth = g_mix.shape[0]
    assert depth == 1
    nb = x_prompt.shape[0]
    nd = x_sample.shape[0]
    half = HEAD_DIM // 2
    inv = ROPE_THETA ** (-jnp.arange(half, dtype=F32) / half)
    inv_row = jnp.tile(inv, LANES // half)[None, :]

    l = 0
    row = lambda v: v[l][None, :]
    w_in_b = w_in[l].astype(BF16)
    w_pa_b = w_pa[l].astype(BF16)
    w_pb_b = w_pb[l].astype(BF16)
    w_o_b = w_o[l].astype(BF16)
    w_up_b = w_up[l].astype(BF16)
    w_dn_b = w_down[l].astype(BF16)
    bs_full = jnp.repeat(b_s[l].T, HEAD_DIM, axis=1)
    ws0 = jnp.repeat(w_s[l][:, 0, 0], HEAD_DIM)[None, :]
    bs0 = bs_full[0:1]

    x1p, kp, vp, gmp = _prompt_mixer(x_prompt, sinks[l], inv_row, row(g_mix), w_in_b, row(ln_v_g), row(ln_v_b),
                                     w_s[l], bs_full, w_pa_b, w_pb_b, w_o_b)
    yp, convp = _prompt_ffn(x1p, row(g_ffn), w_up_b, conv_w[l], row(conv_b), w_dn_b, g_final[None, :])

    xs = x_sample.reshape(nd, D_MODEL)
    ck = cache_swa_k[l].reshape(nd, CHUNK, KV_WIDTH)
    cv = cache_swa_v[l].reshape(nd, CHUNK, KV_WIDTH)
    x1s, gms, ckn, cvn = _sample_mixer(xs, ck, cv, sinks[l], inv_row, row(g_mix), w_in_b, row(ln_v_g),
                                       row(ln_v_b), ws0, bs0, w_pa_b, w_pb_b, w_o_b)
    state = state_ffn_conv[l]
    ys, za, zb = _sample_ffn(x1s, state.reshape(nd, (CONV_W - 1) * 2 * D_FF), row(g_ffn), w_up_b,
                             conv_w[l], row(conv_b), w_dn_b, g_final[None, :])
    conv_s = jnp.stack([state[:, 1, :], jnp.concatenate([za, zb], axis=1)], axis=1)

    return (yp,
            ys.reshape(nd, 1, D_MODEL),
            kp.reshape(1, nb, CHUNK, KV_HEADS, HEAD_DIM),
            vp.reshape(1, nb, CHUNK, KV_HEADS, HEAD_DIM),
            gmp.reshape(1, nb, CHUNK, GM_HEADS, GM_WIDTH // GM_HEADS),
            convp[None],
            ckn.reshape(1, nd, CHUNK, KV_HEADS, HEAD_DIM),
            cvn.reshape(1, nd, CHUNK, KV_HEADS, HEAD_DIM),
            gms.reshape(1, nd, 1, GM_HEADS, GM_WIDTH // GM_HEADS),
            conv_s[None])
```

```python
import functools
import math

import jax
import jax.numpy as jnp
from jax import lax
from jax.experimental import pallas as pl
from jax.experimental.pallas import tpu as pltpu

F32 = jnp.float32
BF16 = jnp.bfloat16

D_MODEL = 1024
SEQ = 2048
PAST_LEN = 16384
CHUNK = 128
GM_HEADS = 8
GM_WIDTH = 512
N_HEADS = 8
KV_HEADS = 2
HEAD_DIM = 64
Q_WIDTH = N_HEADS * HEAD_DIM
KV_WIDTH = KV_HEADS * HEAD_DIM
D_FF = 3 * D_MODEL
CONV_W = 3
EPS = 1e-6
ROPE_THETA = 10000.0
LANES = 128
SUBLANES = 8
OFF_Q = 2 * GM_WIDTH
OFF_K = OFF_Q + Q_WIDTH
OFF_V = OFF_K + KV_WIDTH
OFF_GA = OFF_V + KV_WIDTH
OFF_GB = OFF_GA + D_MODEL
IN_WIDTH = OFF_GB + D_MODEL

PROMPT_TILE = 512
SAMPLE_CHUNK = 16
SAMPLE_UNROLL = 8
FF_CHUNK = 512
ROW_PHASES = 4
VMEM_LIMIT = 52 * 1024 * 1024

_GELU_C0 = math.sqrt(2.0 / math.pi)
_GELU_C1 = _GELU_C0 * 0.044715


def _gelu(x):
    inner = x * (_GELU_C0 + _GELU_C1 * (x * x))
    hx = 0.5 * x
    return hx + hx * jnp.tanh(inner)


def _rms(x, g):
    ms = jnp.mean(x * x, axis=-1, keepdims=True)
    return x * lax.rsqrt(ms + EPS) * g


def _layer_norm(x, g, b):
    mu = jnp.mean(x, axis=-1, keepdims=True)
    xc = x - mu
    var = jnp.mean(xc * xc, axis=-1, keepdims=True)
    return xc * lax.rsqrt(var + EPS) * g + b


def _sigmoid(x):
    return 1.0 / (1.0 + jnp.exp(-x))


def _dot(a, b):
    return jnp.dot(a, b, preferred_element_type=F32)


def _dot_nt(a, b):
    return lax.dot_general(a, b, (((1,), (1,)), ((), ())), preferred_element_type=F32)


def _rope(xc, cos, sin_signed, first_half):
    partner = jnp.where(first_half, pltpu.roll(xc, LANES - 32, 1), pltpu.roll(xc, 32, 1))
    return xc * cos + partner * sin_signed


def _rope_tables(pos_f32, inv_row, shape):
    lane = lax.broadcasted_iota(jnp.int32, shape, 1)
    ang = pos_f32 * inv_row
    sin = jnp.sin(ang)
    return jnp.cos(ang), jnp.where((lane % HEAD_DIM) < HEAD_DIM // 2, -sin, sin)


def _prompt_mixer_kernel(sinks_ref, x_ref, inv_ref, g_ref, win_ref, lng_ref, lnb_ref, ws_ref, bs_ref,
                         wpa_ref, wpb_ref, wo_ref,
                         x1_ref, klast_ref, vlast_ref, gmlast_ref,
                         cos_ref, sin_ref, wcat_ref, kprev_ref, vprev_ref, h_ref, *, tile):
    b = pl.program_id(0)
    s = pl.program_id(1)
    nblk = tile // CHUNK

    @pl.when((b == 0) & (s == 0))
    def _init_tables():
        def body(i, carry):
            r0 = pl.multiple_of(i * CHUNK, CHUNK)
            pos = (lax.broadcasted_iota(jnp.int32, (CHUNK, LANES), 0) + r0).astype(F32)
            cos, sin_signed = _rope_tables(pos, inv_ref[...], (CHUNK, LANES))
            cos_ref[pl.ds(r0, CHUNK), :] = cos
            sin_ref[pl.ds(r0, CHUNK), :] = sin_signed
            return carry
        lax.fori_loop(0, SEQ // CHUNK, body, 0)
        row = lax.broadcasted_iota(jnp.int32, (CHUNK, CHUNK), 0)
        col = lax.broadcasted_iota(jnp.int32, (CHUNK, CHUNK), 1)
        causal = col <= row
        for j in range(GM_HEADS // 2):
            wcat_ref[j, :, 0:CHUNK] = jnp.where(causal, ws_ref[2 * j], 0.0).astype(BF16)
            wcat_ref[j, :, CHUNK:2 * CHUNK] = jnp.where(causal, ws_ref[2 * j + 1], 0.0).astype(BF16)

    @pl.when(s == 0)
    def _reset_carry():
        kprev_ref[...] = jnp.zeros_like(kprev_ref)
        vprev_ref[...] = jnp.zeros_like(vprev_ref)

    lane = lax.broadcasted_iota(jnp.int32, (CHUNK, LANES), 1)
    low = lane < HEAD_DIM
    low2 = lax.broadcasted_iota(jnp.int32, (2 * CHUNK, LANES), 1) < HEAD_DIM
    first_half_t = (lax.broadcasted_iota(jnp.int32, (tile, LANES), 1) % HEAD_DIM) < HEAD_DIM // 2
    row = lax.broadcasted_iota(jnp.int32, (CHUNK, CHUNK), 0)
    col = lax.broadcasted_iota(jnp.int32, (CHUNK, CHUNK), 1)
    cur_mask = col <= row
    prev_live = (col - row) >= jnp.where(s > 0, 0, 2 * CHUNK)
    prev_inner = col >= row

    x = x_ref[...]
    h_ref[...] = _rms(x, g_ref[...]).astype(BF16)

    zqkv = _dot(h_ref[...], win_ref[:, OFF_Q:OFF_GA])
    zuv = _dot(h_ref[...], win_ref[:, 0:OFF_Q])

    r0 = pl.multiple_of(s * tile, tile)
    cos = cos_ref[pl.ds(r0, tile), :]
    sin_signed = sin_ref[pl.ds(r0, tile), :]
    qcols = [(_rope(zqkv[:, j * LANES:(j + 1) * LANES], cos, sin_signed, first_half_t)
              * (HEAD_DIM ** -0.5)).astype(BF16) for j in range(Q_WIDTH // LANES)]
    krot = _rope(zqkv[:, Q_WIDTH:Q_WIDTH + KV_WIDTH], cos, sin_signed, first_half_t)
    vval = zqkv[:, Q_WIDTH + KV_WIDTH:Q_WIDTH + 2 * KV_WIDTH]

    kp = kprev_ref[...]
    vp = vprev_ref[...]
    scores = {}
    vblks = {}
    for i in range(nblk):
        r = slice(i * CHUNK, (i + 1) * CHUNK)
        kk = jnp.concatenate([kp, krot[r]], axis=0)
        vv = jnp.concatenate([vp, vval[r]], axis=0)
        kk_sw = pltpu.roll(kk, HEAD_DIM, 1)
        vv_sw = pltpu.roll(vv, HEAD_DIM, 1)
        for g in range(KV_HEADS):
            k_lo = jnp.where(low2, kk if g == 0 else kk_sw, 0.0).astype(BF16)
            k_hi = jnp.where(low2, 0.0, kk_sw if g == 0 else kk).astype(BF16)
            v_lo = jnp.where(low2, vv if g == 0 else vv_sw, 0.0).astype(BF16)
            v_hi = jnp.where(low2, 0.0, vv_sw if g == 0 else vv).astype(BF16)
            krhs = jnp.concatenate([k_lo, k_hi], axis=0)
            vblks[i, g] = jnp.concatenate([v_lo, v_hi], axis=0)
            qstack = jnp.concatenate([qcols[2 * g][r], qcols[2 * g + 1][r]], axis=0)
            scores[i, g] = _dot_nt(qstack, krhs)
        kp = krot[r]
        vp = vval[r]
    kprev_ref[...] = kp
    vprev_ref[...] = vp

    gates = _dot(h_ref[...], win_ref[:, OFF_GA:IN_WIDTH])

    a = _gelu(zuv)
    u = a[:, 0:GM_WIDTH]
    vg = _layer_norm(a[:, GM_WIDTH:OFF_Q], lng_ref[...], lnb_ref[...])
    ya_blocks = []
    for c in range(nblk):
        r = slice(c * CHUNK, (c + 1) * CHUNK)
        cols = []
        for j in range(GM_HEADS // 2):
            vcol = vg[r, j * LANES:(j + 1) * LANES]
            rhs = jnp.concatenate([jnp.where(low, vcol, 0.0), jnp.where(low, 0.0, vcol)], axis=0).astype(BF16)
            cols.append(_dot(wcat_ref[j], rhs))
        mix = jnp.concatenate(cols, axis=1) + bs_ref[...]
        ya_blocks.append((u[r] * mix).astype(BF16))
    ya = jnp.concatenate(ya_blocks, axis=0)
    pa = _dot(ya, wpa_ref[...])

    yb_blocks = []
    for i in range(nblk):
        pmask = prev_live if i == 0 else prev_inner
        cols_out = []
        for g in range(KV_HEADS):
            sc = scores[i, g]
            p_rows = []
            rinv_rows = []
            for jj in range(2):
                ps = []
                ls = []
                for par in range(2):
                    hd = 4 * g + 2 * jj + par
                    sh = sc[jj * CHUNK:(jj + 1) * CHUNK, par * 2 * CHUNK:(par + 1) * 2 * CHUNK]
                    sp = jnp.where(pmask, sh[:, 0:CHUNK], -jnp.inf)
                    scur = jnp.where(cur_mask, sh[:, CHUNK:2 * CHUNK], -jnp.inf)
                    sink = sinks_ref[hd]
                    m = jnp.maximum(jnp.max(jnp.maximum(sp, scur), axis=-1, keepdims=True), sink)
                    pp = jnp.exp(sp - m)
                    pc = jnp.exp(scur - m)
                    lsum = jnp.sum(pp + pc, axis=-1, keepdims=True) + jnp.exp(sink - m)
                    ps.append(pp.astype(BF16))
                    ps.append(pc.astype(BF16))
                    ls.append(lsum)
                p_rows.append(jnp.concatenate(ps, axis=1))
                rinv_rows.append(jnp.where(low, 1.0 / ls[0], 1.0 / ls[1]))
            o = _dot(jnp.concatenate(p_rows, axis=0), vblks[i, g])
            cols_out.append(o[0:CHUNK] * rinv_rows[0])
            cols_out.append(o[CHUNK:2 * CHUNK] * rinv_rows[1])
        yb_blocks.append(jnp.concatenate(cols_out, axis=1).astype(BF16))
    yb = jnp.concatenate(yb_blocks, axis=0)

    merged = (_sigmoid(gates[:, 0:D_MODEL]) * pa
              + _sigmoid(gates[:, D_MODEL:2 * D_MODEL]) * _dot(yb, wpb_ref[...]))
    x1_ref[...] = x + _dot(merged.astype(BF16), wo_ref[...])

    @pl.when(s == pl.num_programs(1) - 1)
    def _emit_state():
        klast_ref[...] = krot[tile - CHUNK:tile]
        vlast_ref[...] = vval[tile - CHUNK:tile]
        gmlast_ref[...] = vg[tile - CHUNK:tile]


def _const_spec(shape):
    nd = len(shape)
    return pl.BlockSpec(shape, lambda *_: (0,) * nd, pipeline_mode=pl.Buffered(1))


def _prompt_mixer(x, sinks, inv_row, g_mix, w_in, ln_g, ln_b, w_s, bs_full, w_pa, w_pb, w_o):
    nb, seq, _ = x.shape
    tile = PROMPT_TILE
    grid = (nb, seq // tile)
    row_spec = pl.BlockSpec((None, tile, D_MODEL), lambda b, s: (b, s, 0))
    last = lambda w: pl.BlockSpec((None, CHUNK, w), lambda b, s: (b, 0, 0))
    return pl.pallas_call(
        functools.partial(_prompt_mixer_kernel, tile=tile),
        grid=grid,
        in_specs=[
            pl.BlockSpec(memory_space=pltpu.SMEM),
            row_spec,
            _const_spec((1, LANES)),
            _const_spec((1, D_MODEL)),
            _const_spec((D_MODEL, IN_WIDTH)),
            _const_spec((1, GM_WIDTH)),
            _const_spec((1, GM_WIDTH)),
            _const_spec((GM_HEADS, CHUNK, CHUNK)),
            _const_spec((CHUNK, GM_WIDTH)),
            _const_spec((GM_WIDTH, D_MODEL)),
            _const_spec((Q_WIDTH, D_MODEL)),
            _const_spec((D_MODEL, D_MODEL)),
        ],
        out_specs=[row_spec, last(KV_WIDTH), last(KV_WIDTH), last(GM_WIDTH)],
        out_shape=[
            jax.ShapeDtypeStruct((nb, seq, D_MODEL), F32),
            jax.ShapeDtypeStruct((nb, CHUNK, KV_WIDTH), F32),
            jax.ShapeDtypeStruct((nb, CHUNK, KV_WIDTH), F32),
            jax.ShapeDtypeStruct((nb, CHUNK, GM_WIDTH), F32),
        ],
        scratch_shapes=[
            pltpu.VMEM((SEQ, LANES), F32),
            pltpu.VMEM((SEQ, LANES), F32),
            pltpu.VMEM((GM_HEADS // 2, CHUNK, 2 * CHUNK), BF16),
            pltpu.VMEM((CHUNK, KV_WIDTH), F32),
            pltpu.VMEM((CHUNK, KV_WIDTH), F32),
            pltpu.VMEM((tile, D_MODEL), BF16),
        ],
        compiler_params=pltpu.CompilerParams(
            dimension_semantics=("arbitrary", "arbitrary"), vmem_limit_bytes=VMEM_LIMIT),
        name="prompt_mixer",
    )(sinks, x, inv_row, g_mix, w_in, ln_g, ln_b, w_s, bs_full, w_pa, w_pb, w_o)


def _prompt_ffn_kernel(x_ref, g_ref, wup_ref, cw_ref, cb_ref, wdn_ref, gf_ref,
                       y_ref, convlast_ref, zbuf_ref, gbuf_ref, ybuf_ref, h_ref, *, tile):
    s = pl.program_id(1)
    pad = SUBLANES
    q = tile // ROW_PHASES
    nslab_half = D_FF // LANES

    @pl.when(s == 0)
    def _reset_carry():
        zbuf_ref[:, 0:pad, :] = jnp.zeros((2 * nslab_half, pad, LANES), F32)

    x = x_ref[...]
    h_ref[...] = _rms(x, g_ref[...]).astype(BF16)

    def up_project(j):
        for half in range(2):
            n0 = half * D_FF + j * FF_CHUNK
            z = _dot(h_ref[...], wup_ref[:, n0:n0 + FF_CHUNK])
            for jj in range(FF_CHUNK // LANES):
                zbuf_ref[n0 // LANES + jj, pad:pad + tile, :] = z[:, jj * LANES:(jj + 1) * LANES]

    def conv(slab):
        cs = slice(slab * LANES, (slab + 1) * LANES)
        rows = [zbuf_ref[slab, pl.ds(pad - (CONV_W - 1) + m, q, stride=ROW_PHASES), :]
                for m in range(ROW_PHASES + CONV_W - 1)]
        taps = [cw_ref[t:t + 1, cs] for t in range(CONV_W)]
        bias = cb_ref[:, cs]
        return [bias + taps[0] * rows[r] + taps[1] * rows[r + 1] + taps[2] * rows[r + 2]
                for r in range(ROW_PHASES)]

    def gate_and_project(j, yperm):
        for ja in range(j * (FF_CHUNK // LANES), (j + 1) * (FF_CHUNK // LANES)):
            ca = conv(ja)
            cb = conv(ja + nslab_half)
            for r in range(ROW_PHASES):
                gbuf_ref[r * q:(r + 1) * q, ja * LANES:(ja + 1) * LANES] = (_gelu(ca[r]) * cb[r]).astype(BF16)
        cs = slice(j * FF_CHUNK, (j + 1) * FF_CHUNK)
        part = _dot(gbuf_ref[:, cs], wdn_ref[cs, :])
        return part if yperm is None else yperm + part

    nchunk = D_FF // FF_CHUNK
    yperm = None
    up_project(0)
    for j in range(1, nchunk):
        up_project(j)
        yperm = gate_and_project(j - 1, yperm)
    yperm = gate_and_project(nchunk - 1, yperm)

    for j in range(D_MODEL // LANES):
        for r in range(ROW_PHASES):
            ybuf_ref[j, pl.ds(r, q, stride=ROW_PHASES), :] = yperm[r * q:(r + 1) * q, j * LANES:(j + 1) * LANES]
    y = jnp.concatenate([ybuf_ref[j] for j in range(D_MODEL // LANES)], axis=1)
    y_ref[...] = _rms(x + y, gf_ref[...])

    @pl.when(s == pl.num_programs(1) - 1)
    def _emit_state():
        for slab in range(2 * nslab_half):
            convlast_ref[:, slab * LANES:(slab + 1) * LANES] = (
                zbuf_ref[slab, pad + tile - (CONV_W - 1):pad + tile, :])

    zbuf_ref[:, 0:pad, :] = zbuf_ref[:, tile:tile + pad, :]


def _prompt_ffn(x1, g_ffn, w_up, conv_w, conv_b, w_down, g_final):
    nb, seq, _ = x1.shape
    tile = PROMPT_TILE
    row_spec = pl.BlockSpec((None, tile, D_MODEL), lambda b, s: (b, s, 0))
    return pl.pallas_call(
        functools.partial(_prompt_ffn_kernel, tile=tile),
        grid=(nb, seq // tile),
        in_specs=[
            row_spec,
            _const_spec((1, D_MODEL)),
            _const_spec((D_MODEL, 2 * D_FF)),
            _const_spec((CONV_W, 2 * D_FF)),
            _const_spec((1, 2 * D_FF)),
            _const_spec((D_FF, D_MODEL)),
            _const_spec((1, D_MODEL)),
        ],
        out_specs=[row_spec, pl.BlockSpec((None, CONV_W - 1, 2 * D_FF), lambda b, s: (b, 0, 0))],
        out_shape=[
            jax.ShapeDtypeStruct((nb, seq, D_MODEL), F32),
            jax.ShapeDtypeStruct((nb, CONV_W - 1, 2 * D_FF), F32),
        ],
        scratch_shapes=[
            pltpu.VMEM((2 * D_FF // LANES, tile + SUBLANES, LANES), F32),
            pltpu.VMEM((tile, D_FF), BF16),
            pltpu.VMEM((D_MODEL // LANES, tile, LANES), F32),
            pltpu.VMEM((tile, D_MODEL), BF16),
        ],
        compiler_params=pltpu.CompilerParams(
            dimension_semantics=("arbitrary", "arbitrary"), vmem_limit_bytes=VMEM_LIMIT),
        name="prompt_ffn",
    )(x1, g_ffn, w_up, conv_w, conv_b, w_down, g_final)


def _sample_mixer_kernel(sinks_ref, x_ref, ck_ref, cv_ref, inv_ref, g_ref, win_ref, lng_ref, lnb_ref,
                         ws0_ref, bs0_ref, wpa_ref, wpb_ref, wo_ref,
                         x1_ref, gm_ref, ckn_ref, cvn_ref,
                         qz_ref, kn_ref, vn_ref, o_ref, *, chunk):
    lane = lax.broadcasted_iota(jnp.int32, (chunk, LANES), 1)
    low = lane < HEAD_DIM
    first_half = (lane % HEAD_DIM) < HEAD_DIM // 2

    x = x_ref[...]
    h = _rms(x, g_ref[...]).astype(BF16)
    a = _gelu(_dot(h, win_ref[:, 0:OFF_Q]))
    u = a[:, 0:GM_WIDTH]
    vg = _layer_norm(a[:, GM_WIDTH:OFF_Q], lng_ref[...], lnb_ref[...])
    gm_ref[...] = vg
    ya = (u * (vg * ws0_ref[...] + bs0_ref[...])).astype(BF16)

    zqkv = _dot(h, win_ref[:, OFF_Q:OFF_GA])
    pos = jnp.full((chunk, LANES), float(PAST_LEN), F32)
    cos, sin_signed = _rope_tables(pos, inv_ref[...], (chunk, LANES))
    knew = _rope(zqkv[:, Q_WIDTH:Q_WIDTH + KV_WIDTH], cos, sin_signed, first_half)
    vnew = zqkv[:, Q_WIDTH + KV_WIDTH:Q_WIDTH + 2 * KV_WIDTH]
    kn_ref[...] = knew
    vn_ref[...] = vnew
    for j in range(Q_WIDTH // LANES):
        qc = _rope(zqkv[:, j * LANES:(j + 1) * LANES], cos, sin_signed, first_half) * (HEAD_DIM ** -0.5)
        qsw = pltpu.roll(qc, HEAD_DIM, 1)
        g = j // 2
        even = jnp.where(low, qc, 0.0) if g == 0 else jnp.where(low, 0.0, qsw)
        odd = jnp.where(low, qsw, 0.0) if g == 0 else jnp.where(low, 0.0, qc)
        qz_ref[(2 * j) * chunk:(2 * j + 1) * chunk, :] = even
        qz_ref[(2 * j + 1) * chunk:(2 * j + 2) * chunk, :] = odd

    hrow = lax.broadcasted_iota(jnp.int32, (N_HEADS, 1), 0)
    sink_col = jnp.zeros((N_HEADS, 1), F32)
    for hd in range(N_HEADS):
        sink_col = jnp.where(hrow == hd, sinks_ref[hd], sink_col)
    head_low = lax.broadcasted_iota(jnp.int32, (N_HEADS, LANES), 0) < N_HEADS // KV_HEADS
    lane_low = lax.broadcasted_iota(jnp.int32, (N_HEADS, LANES), 1) < HEAD_DIM
    own_half = head_low == lane_low
    krow = lax.broadcasted_iota(jnp.int32, (CHUNK, LANES), 0)

    def body(bi, carry):
        qz = qz_ref[pl.ds(bi, N_HEADS, stride=chunk), :]
        kb = ck_ref[bi]
        vb = cv_ref[bi]
        kn = kn_ref[pl.ds(bi, 1), :]
        vn = vn_ref[pl.ds(bi, 1), :]
        sc = _dot_nt(qz.astype(BF16), kb.astype(BF16))
        sn = jnp.sum(qz * kn, axis=-1, keepdims=True)
        m = jnp.maximum(jnp.maximum(jnp.max(sc, axis=-1, keepdims=True), sn), sink_col)
        p = jnp.exp(sc - m)
        pn = jnp.exp(sn - m)
        lsum = jnp.sum(p, axis=-1, keepdims=True) + pn + jnp.exp(sink_col - m)
        o = (_dot(p.astype(BF16), vb.astype(BF16)) + pn * vn) / lsum
        o_ref[pl.ds(bi, N_HEADS, stride=chunk), :] = jnp.where(own_half, o, 0.0)
        ckn_ref[bi] = jnp.where(krow == CHUNK - 1, kn, pltpu.roll(kb, CHUNK - 1, 0))
        cvn_ref[bi] = jnp.where(krow == CHUNK - 1, vn, pltpu.roll(vb, CHUNK - 1, 0))
        return carry

    lax.fori_loop(0, chunk, body, 0, unroll=SAMPLE_UNROLL)

    cols = []
    for j in range(Q_WIDTH // LANES):
        g = j // 2
        oe = o_ref[(2 * j) * chunk:(2 * j + 1) * chunk, :]
        oo = o_ref[(2 * j + 1) * chunk:(2 * j + 2) * chunk, :]
        if g == 0:
            cols.append(jnp.where(low, oe, pltpu.roll(oo, HEAD_DIM, 1)))
        else:
            cols.append(jnp.where(low, pltpu.roll(oe, HEAD_DIM, 1), oo))
    yb = jnp.concatenate(cols, axis=1).astype(BF16)

    gates = _dot(h, win_ref[:, OFF_GA:IN_WIDTH])
    merged = (_sigmoid(gates[:, 0:D_MODEL]) * _dot(ya, wpa_ref[...])
              + _sigmoid(gates[:, D_MODEL:2 * D_MODEL]) * _dot(yb, wpb_ref[...]))
    x1_ref[...] = x + _dot(merged.astype(BF16), wo_ref[...])


def _sample_mixer(x, ck, cv, sinks, inv_row, g_mix, w_in, ln_g, ln_b, ws0, bs0, w_pa, w_pb, w_o):
    n = x.shape[0]
    chunk = SAMPLE_CHUNK
    row = lambda w: pl.BlockSpec((chunk, w), lambda i: (i, 0))
    cache = pl.BlockSpec((chunk, CHUNK, KV_WIDTH), lambda i: (i, 0, 0))
    return pl.pallas_call(
        functools.partial(_sample_mixer_kernel, chunk=chunk),
        grid=(n // chunk,),
        in_specs=[
            pl.BlockSpec(memory_space=pltpu.SMEM),
            row(D_MODEL), cache, cache,
            _const_spec((1, LANES)),
            _const_spec((1, D_MODEL)),
            _const_spec((D_MODEL, IN_WIDTH)),
            _const_spec((1, GM_WIDTH)),
            _const_spec((1, GM_WIDTH)),
            _const_spec((1, GM_WIDTH)),
            _const_spec((1, GM_WIDTH)),
            _const_spec((GM_WIDTH, D_MODEL)),
            _const_spec((Q_WIDTH, D_MODEL)),
            _const_spec((D_MODEL, D_MODEL)),
        ],
        out_specs=[row(D_MODEL), row(GM_WIDTH), cache, cache],
        out_shape=[
            jax.ShapeDtypeStruct((n, D_MODEL), F32),
            jax.ShapeDtypeStruct((n, GM_WIDTH), F32),
            jax.ShapeDtypeStruct((n, CHUNK, KV_WIDTH), F32),
            jax.ShapeDtypeStruct((n, CHUNK, KV_WIDTH), F32),
        ],
        scratch_shapes=[
            pltpu.VMEM((N_HEADS * chunk, LANES), F32),
            pltpu.VMEM((chunk, KV_WIDTH), F32),
            pltpu.VMEM((chunk, KV_WIDTH), F32),
            pltpu.VMEM((N_HEADS * chunk, LANES), F32),
        ],
        compiler_params=pltpu.CompilerParams(
            dimension_semantics=("arbitrary",), vmem_limit_bytes=VMEM_LIMIT),
        name="sample_mixer",
    )(sinks, x, ck, cv, inv_row, g_mix, w_in, ln_g, ln_b, ws0, bs0, w_pa, w_pb, w_o)


def _sample_ffn_kernel(x_ref, g_ref, wua_ref, wub_ref, cwa_ref, cwb_ref, cba_ref, cbb_ref,
                       p0a_ref, p0b_ref, p1a_ref, p1b_ref, wdn_ref, gf_ref,
                       y_ref, za_ref, zb_ref, acc_ref):
    j = pl.program_id(0)
    x = x_ref[...]
    h = _rms(x, g_ref[...]).astype(BF16)
    za = _dot(h, wua_ref[...])
    zb = _dot(h, wub_ref[...])
    za_ref[...] = za
    zb_ref[...] = zb
    ca = cba_ref[...] + cwa_ref[0:1, :] * p0a_ref[...] + cwa_ref[1:2, :] * p1a_ref[...] + cwa_ref[2:3, :] * za
    cb = cbb_ref[...] + cwb_ref[0:1, :] * p0b_ref[...] + cwb_ref[1:2, :] * p1b_ref[...] + cwb_ref[2:3, :] * zb
    part = _dot((_gelu(ca) * cb).astype(BF16), wdn_ref[...])

    @pl.when(j == 0)
    def _first():
        acc_ref[...] = part

    @pl.when(j > 0)
    def _rest():
        acc_ref[...] += part

    @pl.when(j == pl.num_programs(0) - 1)
    def _finish():
        y_ref[...] = _rms(x + acc_ref[...], gf_ref[...])


def _sample_ffn(x1, state2d, g_ffn, w_up, conv_w, conv_b, w_down, g_final):
    n = x1.shape[0]
    nchunk = D_FF // FF_CHUNK
    full = lambda w: pl.BlockSpec((n, w), lambda j: (0, 0))
    col = lambda rows, off: pl.BlockSpec((rows, FF_CHUNK), lambda j, off=off: (0, j + off))
    return pl.pallas_call(
        _sample_ffn_kernel,
        grid=(nchunk,),
        in_specs=[
            full(D_MODEL),
            pl.BlockSpec((1, D_MODEL), lambda j: (0, 0)),
            col(D_MODEL, 0), col(D_MODEL, nchunk),
            col(CONV_W, 0), col(CONV_W, nchunk),
            col(1, 0), col(1, nchunk),
            col(n, 0), col(n, nchunk), col(n, 2 * nchunk), col(n, 3 * nchunk),
            pl.BlockSpec((FF_CHUNK, D_MODEL), lambda j: (j, 0)),
            pl.BlockSpec((1, D_MODEL), lambda j: (0, 0)),
        ],
        out_specs=[full(D_MODEL), col(n, 0), col(n, 0)],
        out_shape=[
            jax.ShapeDtypeStruct((n, D_MODEL), F32),
            jax.ShapeDtypeStruct((n, D_FF), F32),
            jax.ShapeDtypeStruct((n, D_FF), F32),
        ],
        scratch_shapes=[pltpu.VMEM((n, D_MODEL), F32)],
        compiler_params=pltpu.CompilerParams(
            dimension_semantics=("arbitrary",), vmem_limit_bytes=VMEM_LIMIT),
        name="sample_ffn",
    )(x1, g_ffn, w_up, w_up, conv_w, conv_w, conv_b, conv_b,
      state2d, state2d, state2d, state2d, w_down, g_final)


def kernel(x_prompt, x_sample, cache_swa_k, cache_swa_v, state_ffn_conv, g_mix, w_in, ln_v_g, ln_v_b,
           w_s, b_s, sinks, w_pa, w_pb, w_o, g_ffn, w_up, conv_w, conv_b, w_down, g_final):
    depth = g_mix.shape[0]
    assert depth == 1
    nb = x_prompt.shape[0]
    nd = x_sample.shape[0]
    half = HEAD_DIM // 2
    inv = ROPE_THETA ** (-jnp.arange(half, dtype=F32) / half)
    inv_row = jnp.tile(inv, LANES // half)[None, :]

    l = 0
    row = lambda v: v[l][None, :]
    w_in_b = w_in[l].astype(BF16)
    w_pa_b = w_pa[l].astype(BF16)
    w_pb_b = w_pb[l].astype(BF16)
    w_o_b = w_o[l].astype(BF16)
    w_up_b = w_up[l].astype(BF16)
    w_dn_b = w_down[l].astype(BF16)
    bs_full = jnp.repeat(b_s[l].T, HEAD_DIM, axis=1)
    ws0 = jnp.repeat(w_s[l][:, 0, 0], HEAD_DIM)[None, :]
    bs0 = bs_full[0:1]

    x1p, kp, vp, gmp = _prompt_mixer(x_prompt, sinks[l], inv_row, row(g_mix), w_in_b, row(ln_v_g), row(ln_v_b),
                                     w_s[l], bs_full, w_pa_b, w_pb_b, w_o_b)
    yp, convp = _prompt_ffn(x1p, row(g_ffn), w_up_b, conv_w[l], row(conv_b), w_dn_b, g_final[None, :])

    xs = x_sample.reshape(nd, D_MODEL)
    ck = cache_swa_k[l].reshape(nd, CHUNK, KV_WIDTH)
    cv = cache_swa_v[l].reshape(nd, CHUNK, KV_WIDTH)
    x1s, gms, ckn, cvn = _sample_mixer(xs, ck, cv, sinks[l], inv_row, row(g_mix), w_in_b, row(ln_v_g),
                                       row(ln_v_b), ws0, bs0, w_pa_b, w_pb_b, w_o_b)
    state = state_ffn_conv[l]
    ys, za, zb = _sample_ffn(x1s, state.reshape(nd, (CONV_W - 1) * 2 * D_FF), row(g_ffn), w_up_b,
                             conv_w[l], row(conv_b), w_dn_b, g_final[None, :])
    conv_s = jnp.stack([state[:, 1, :], jnp.concatenate([za, zb], axis=1)], axis=1)

    return (yp,
            ys.reshape(nd, 1, D_MODEL),
            kp.reshape(1, nb, CHUNK, KV_HEADS, HEAD_DIM),
            vp.reshape(1, nb, CHUNK, KV_HEADS, HEAD_DIM),
            gmp.reshape(1, nb, CHUNK, GM_HEADS, GM_WIDTH // GM_HEADS),
            convp[None],
            ckn.reshape(1, nd, CHUNK, KV_HEADS, HEAD_DIM),
            cvn.reshape(1, nd, CHUNK, KV_HEADS, HEAD_DIM),
            gms.reshape(1, nd, 1, GM_HEADS, GM_WIDTH // GM_HEADS),
            conv_s[None])
```

```python
import functools
import math

import jax
import jax.numpy as jnp
from jax import lax
from jax.experimental import pallas as pl
from jax.experimental.pallas import tpu as pltpu

F32 = jnp.float32
BF16 = jnp.bfloat16

D_MODEL = 1024
SEQ = 2048
PAST_LEN = 16384
CHUNK = 128
GM_HEADS = 8
GM_WIDTH = 512
N_HEADS = 8
KV_HEADS = 2
HEAD_DIM = 64
Q_WIDTH = N_HEADS * HEAD_DIM
KV_WIDTH = KV_HEADS * HEAD_DIM
D_FF = 3 * D_MODEL
CONV_W = 3
EPS = 1e-6
ROPE_THETA = 10000.0
LANES = 128
SUBLANES = 8
OFF_Q = 2 * GM_WIDTH
OFF_K = OFF_Q + Q_WIDTH
OFF_V = OFF_K + KV_WIDTH
OFF_GA = OFF_V + KV_WIDTH
OFF_GB = OFF_GA + D_MODEL
IN_WIDTH = OFF_GB + D_MODEL

PROMPT_TILE = 512
SAMPLE_CHUNK = 16
SAMPLE_UNROLL = 8
FF_CHUNK = 512
FFN_LAG = 4
ROW_PHASES = 4
VMEM_LIMIT = 52 * 1024 * 1024

_GELU_C0 = math.sqrt(2.0 / math.pi)
_GELU_C1 = _GELU_C0 * 0.044715


def _gelu(x):
    inner = x * (_GELU_C0 + _GELU_C1 * (x * x))
    hx = 0.5 * x
    return hx + hx * jnp.tanh(inner)


def _rms(x, g):
    ms = jnp.mean(x * x, axis=-1, keepdims=True)
    return x * lax.rsqrt(ms + EPS) * g


def _layer_norm(x, g, b):
    mu = jnp.mean(x, axis=-1, keepdims=True)
    xc = x - mu
    var = jnp.mean(xc * xc, axis=-1, keepdims=True)
    return xc * lax.rsqrt(var + EPS) * g + b


def _sigmoid(x):
    return 1.0 / (1.0 + jnp.exp(-x))


def _dot(a, b):
    return jnp.dot(a, b, preferred_element_type=F32)


def _dot_nt(a, b):
    return lax.dot_general(a, b, (((1,), (1,)), ((), ())), preferred_element_type=F32)


def _rope(xc, cos, sin_signed, first_half):
    partner = jnp.where(first_half, pltpu.roll(xc, LANES - 32, 1), pltpu.roll(xc, 32, 1))
    return xc * cos + partner * sin_signed


def _rope_tables(pos_f32, inv_row, shape):
    lane = lax.broadcasted_iota(jnp.int32, shape, 1)
    ang = pos_f32 * inv_row
    sin = jnp.sin(ang)
    return jnp.cos(ang), jnp.where((lane % HEAD_DIM) < HEAD_DIM // 2, -sin, sin)


def _prompt_mixer_kernel(sinks_ref, x_ref, inv_ref, g_ref, win_ref, lng_ref, lnb_ref, ws_ref, bs_ref,
                         wpa_ref, wpb_ref, wo_ref,
                         x1_ref, klast_ref, vlast_ref, gmlast_ref,
                         cos_ref, sin_ref, wcat_ref, kprev_ref, vprev_ref, h_ref, *, tile):
    b = pl.program_id(0)
    s = pl.program_id(1)
    nblk = tile // CHUNK

    @pl.when((b == 0) & (s == 0))
    def _init_tables():
        def body(i, carry):
            r0 = pl.multiple_of(i * CHUNK, CHUNK)
            pos = (lax.broadcasted_iota(jnp.int32, (CHUNK, LANES), 0) + r0).astype(F32)
            cos, sin_signed = _rope_tables(pos, inv_ref[...], (CHUNK, LANES))
            cos_ref[pl.ds(r0, CHUNK), :] = cos
            sin_ref[pl.ds(r0, CHUNK), :] = sin_signed
            return carry
        lax.fori_loop(0, SEQ // CHUNK, body, 0)
        row = lax.broadcasted_iota(jnp.int32, (CHUNK, CHUNK), 0)
        col = lax.broadcasted_iota(jnp.int32, (CHUNK, CHUNK), 1)
        causal = col <= row
        for j in range(GM_HEADS // 2):
            wcat_ref[j, :, 0:CHUNK] = jnp.where(causal, ws_ref[2 * j], 0.0).astype(BF16)
            wcat_ref[j, :, CHUNK:2 * CHUNK] = jnp.where(causal, ws_ref[2 * j + 1], 0.0).astype(BF16)

    @pl.when(s == 0)
    def _reset_carry():
        kprev_ref[...] = jnp.zeros_like(kprev_ref)
        vprev_ref[...] = jnp.zeros_like(vprev_ref)

    lane = lax.broadcasted_iota(jnp.int32, (CHUNK, LANES), 1)
    low = lane < HEAD_DIM
    low2 = lax.broadcasted_iota(jnp.int32, (2 * CHUNK, LANES), 1) < HEAD_DIM
    first_half_t = (lax.broadcasted_iota(jnp.int32, (tile, LANES), 1) % HEAD_DIM) < HEAD_DIM // 2
    row = lax.broadcasted_iota(jnp.int32, (CHUNK, CHUNK), 0)
    col = lax.broadcasted_iota(jnp.int32, (CHUNK, CHUNK), 1)
    cur_mask = col <= row
    prev_live = (col - row) >= jnp.where(s > 0, 0, 2 * CHUNK)
    prev_inner = col >= row

    x = x_ref[...]
    h_ref[...] = _rms(x, g_ref[...]).astype(BF16)

    zqkv = _dot(h_ref[...], win_ref[:, OFF_Q:OFF_GA])
    zuv = _dot(h_ref[...], win_ref[:, 0:OFF_Q])

    r0 = pl.multiple_of(s * tile, tile)
    cos = cos_ref[pl.ds(r0, tile), :]
    sin_signed = sin_ref[pl.ds(r0, tile), :]
    qcols = [(_rope(zqkv[:, j * LANES:(j + 1) * LANES], cos, sin_signed, first_half_t)
              * (HEAD_DIM ** -0.5)).astype(BF16) for j in range(Q_WIDTH // LANES)]
    krot = _rope(zqkv[:, Q_WIDTH:Q_WIDTH + KV_WIDTH], cos, sin_signed, first_half_t)
    vval = zqkv[:, Q_WIDTH + KV_WIDTH:Q_WIDTH + 2 * KV_WIDTH]

    kp = kprev_ref[...]
    vp = vprev_ref[...]
    scores = {}
    vblks = {}
    for i in range(nblk):
        r = slice(i * CHUNK, (i + 1) * CHUNK)
        kk = jnp.concatenate([kp, krot[r]], axis=0)
        vv = jnp.concatenate([vp, vval[r]], axis=0)
        kk_sw = pltpu.roll(kk, HEAD_DIM, 1)
        vv_sw = pltpu.roll(vv, HEAD_DIM, 1)
        for g in range(KV_HEADS):
            k_lo = jnp.where(low2, kk if g == 0 else kk_sw, 0.0).astype(BF16)
            k_hi = jnp.where(low2, 0.0, kk_sw if g == 0 else kk).astype(BF16)
            v_lo = jnp.where(low2, vv if g == 0 else vv_sw, 0.0).astype(BF16)
            v_hi = jnp.where(low2, 0.0, vv_sw if g == 0 else vv).astype(BF16)
            krhs = jnp.concatenate([k_lo, k_hi], axis=0)
            vblks[i, g] = jnp.concatenate([v_lo, v_hi], axis=0)
            qstack = jnp.concatenate([qcols[2 * g][r], qcols[2 * g + 1][r]], axis=0)
            scores[i, g] = _dot_nt(qstack, krhs)
        kp = krot[r]
        vp = vval[r]
    kprev_ref[...] = kp
    vprev_ref[...] = vp

    gates = _dot(h_ref[...], win_ref[:, OFF_GA:IN_WIDTH])

    a = _gelu(zuv)
    u = a[:, 0:GM_WIDTH]
    vg = _layer_norm(a[:, GM_WIDTH:OFF_Q], lng_ref[...], lnb_ref[...])
    ya_blocks = []
    for c in range(nblk):
        r = slice(c * CHUNK, (c + 1) * CHUNK)
        cols = []
        for j in range(GM_HEADS // 2):
            vcol = vg[r, j * LANES:(j + 1) * LANES]
            rhs = jnp.concatenate([jnp.where(low, vcol, 0.0), jnp.where(low, 0.0, vcol)], axis=0).astype(BF16)
            cols.append(_dot(wcat_ref[j], rhs))
        mix = jnp.concatenate(cols, axis=1) + bs_ref[...]
        ya_blocks.append((u[r] * mix).astype(BF16))
    ya = jnp.concatenate(ya_blocks, axis=0)
    pa = _dot(ya, wpa_ref[...])

    yb_blocks = []
    for i in range(nblk):
        pmask = prev_live if i == 0 else prev_inner
        cols_out = []
        for g in range(KV_HEADS):
            sc = scores[i, g]
            p_rows = []
            rinv_rows = []
            for jj in range(2):
                ps = []
                ls = []
                for par in range(2):
                    hd = 4 * g + 2 * jj + par
                    sh = sc[jj * CHUNK:(jj + 1) * CHUNK, par * 2 * CHUNK:(par + 1) * 2 * CHUNK]
                    sp = jnp.where(pmask, sh[:, 0:CHUNK], -jnp.inf)
                    scur = jnp.where(cur_mask, sh[:, CHUNK:2 * CHUNK], -jnp.inf)
                    sink = sinks_ref[hd]
                    m = jnp.maximum(jnp.max(jnp.maximum(sp, scur), axis=-1, keepdims=True), sink)
                    pp = jnp.exp(sp - m)
                    pc = jnp.exp(scur - m)
                    lsum = jnp.sum(pp + pc, axis=-1, keepdims=True) + jnp.exp(sink - m)
                    ps.append(pp.astype(BF16))
                    ps.append(pc.astype(BF16))
                    ls.append(lsum)
                p_rows.append(jnp.concatenate(ps, axis=1))
                rinv_rows.append(jnp.where(low, 1.0 / ls[0], 1.0 / ls[1]))
            o = _dot(jnp.concatenate(p_rows, axis=0), vblks[i, g])
            cols_out.append(o[0:CHUNK] * rinv_rows[0])
            cols_out.append(o[CHUNK:2 * CHUNK] * rinv_rows[1])
        yb_blocks.append(jnp.concatenate(cols_out, axis=1).astype(BF16))
    yb = jnp.concatenate(yb_blocks, axis=0)

    merged = (_sigmoid(gates[:, 0:D_MODEL]) * pa
              + _sigmoid(gates[:, D_MODEL:2 * D_MODEL]) * _dot(yb, wpb_ref[...]))
    x1_ref[...] = x + _dot(merged.astype(BF16), wo_ref[...])

    @pl.when(s == pl.num_programs(1) - 1)
    def _emit_state():
        klast_ref[...] = krot[tile - CHUNK:tile]
        vlast_ref[...] = vval[tile - CHUNK:tile]
        gmlast_ref[...] = vg[tile - CHUNK:tile]


def _const_spec(shape):
    nd = len(shape)
    return pl.BlockSpec(shape, lambda *_: (0,) * nd, pipeline_mode=pl.Buffered(1))


def _prompt_mixer(x, sinks, inv_row, g_mix, w_in, ln_g, ln_b, w_s, bs_full, w_pa, w_pb, w_o):
    nb, seq, _ = x.shape
    tile = PROMPT_TILE
    grid = (nb, seq // tile)
    row_spec = pl.BlockSpec((None, tile, D_MODEL), lambda b, s: (b, s, 0))
    last = lambda w: pl.BlockSpec((None, CHUNK, w), lambda b, s: (b, 0, 0))
    return pl.pallas_call(
        functools.partial(_prompt_mixer_kernel, tile=tile),
        grid=grid,
        in_specs=[
            pl.BlockSpec(memory_space=pltpu.SMEM),
            row_spec,
            _const_spec((1, LANES)),
            _const_spec((1, D_MODEL)),
            _const_spec((D_MODEL, IN_WIDTH)),
            _const_spec((1, GM_WIDTH)),
            _const_spec((1, GM_WIDTH)),
            _const_spec((GM_HEADS, CHUNK, CHUNK)),
            _const_spec((CHUNK, GM_WIDTH)),
            _const_spec((GM_WIDTH, D_MODEL)),
            _const_spec((Q_WIDTH, D_MODEL)),
            _const_spec((D_MODEL, D_MODEL)),
        ],
        out_specs=[row_spec, last(KV_WIDTH), last(KV_WIDTH), last(GM_WIDTH)],
        out_shape=[
            jax.ShapeDtypeStruct((nb, seq, D_MODEL), F32),
            jax.ShapeDtypeStruct((nb, CHUNK, KV_WIDTH), F32),
            jax.ShapeDtypeStruct((nb, CHUNK, KV_WIDTH), F32),
            jax.ShapeDtypeStruct((nb, CHUNK, GM_WIDTH), F32),
        ],
        scratch_shapes=[
            pltpu.VMEM((SEQ, LANES), F32),
            pltpu.VMEM((SEQ, LANES), F32),
            pltpu.VMEM((GM_HEADS // 2, CHUNK, 2 * CHUNK), BF16),
            pltpu.VMEM((CHUNK, KV_WIDTH), F32),
            pltpu.VMEM((CHUNK, KV_WIDTH), F32),
            pltpu.VMEM((tile, D_MODEL), BF16),
        ],
        compiler_params=pltpu.CompilerParams(
            dimension_semantics=("arbitrary", "arbitrary"), vmem_limit_bytes=VMEM_LIMIT),
        name="prompt_mixer",
    )(sinks, x, inv_row, g_mix, w_in, ln_g, ln_b, w_s, bs_full, w_pa, w_pb, w_o)


def _prompt_ffn_kernel(x_ref, g_ref, wup_ref, cw_ref, cb_ref, wdn_ref, gf_ref,
                       y_ref, convlast_ref, zbuf_ref, gbuf_ref, ybuf_ref, h_ref, *, tile):
    s = pl.program_id(1)
    pad = SUBLANES
    q = tile // ROW_PHASES
    nslab_half = D_FF // LANES

    @pl.when(s == 0)
    def _reset_carry():
        zbuf_ref[:, 0:pad, :] = jnp.zeros((2 * nslab_half, pad, LANES), F32)

    x = x_ref[...]
    h_ref[...] = _rms(x, g_ref[...]).astype(BF16)

    def up_project(j):
        for half in range(2):
            n0 = half * D_FF + j * FF_CHUNK
            z = _dot(h_ref[...], wup_ref[:, n0:n0 + FF_CHUNK])
            for jj in range(FF_CHUNK // LANES):
                zbuf_ref[n0 // LANES + jj, pad:pad + tile, :] = z[:, jj * LANES:(jj + 1) * LANES]

    def conv(slab):
        cs = slice(slab * LANES, (slab + 1) * LANES)
        rows = [zbuf_ref[slab, pl.ds(pad - (CONV_W - 1) + m, q, stride=ROW_PHASES), :]
                for m in range(ROW_PHASES + CONV_W - 1)]
        taps = [cw_ref[t:t + 1, cs] for t in range(CONV_W)]
        bias = cb_ref[:, cs]
        return [bias + taps[0] * rows[r] + taps[1] * rows[r + 1] + taps[2] * rows[r + 2]
                for r in range(ROW_PHASES)]

    def gate_and_project(j, yperm):
        for ja in range(j * (FF_CHUNK // LANES), (j + 1) * (FF_CHUNK // LANES)):
            ca = conv(ja)
            cb = conv(ja + nslab_half)
            for r in range(ROW_PHASES):
                gbuf_ref[r * q:(r + 1) * q, ja * LANES:(ja + 1) * LANES] = (_gelu(ca[r]) * cb[r]).astype(BF16)
        cs = slice(j * FF_CHUNK, (j + 1) * FF_CHUNK)
        part = _dot(gbuf_ref[:, cs], wdn_ref[cs, :])
        return part if yperm is None else yperm + part

    nchunk = D_FF // FF_CHUNK
    yperm = None
    for j in range(nchunk + FFN_LAG):
        if j < nchunk:
            up_project(j)
        if j >= FFN_LAG:
            yperm = gate_and_project(j - FFN_LAG, yperm)

    for j in range(D_MODEL // LANES):
        for r in range(ROW_PHASES):
            ybuf_ref[j, pl.ds(r, q, stride=ROW_PHASES), :] = yperm[r * q:(r + 1) * q, j * LANES:(j + 1) * LANES]
    y = jnp.concatenate([ybuf_ref[j] for j in range(D_MODEL // LANES)], axis=1)
    y_ref[...] = _rms(x + y, gf_ref[...])

    @pl.when(s == pl.num_programs(1) - 1)
    def _emit_state():
        for slab in range(2 * nslab_half):
            convlast_ref[:, slab * LANES:(slab + 1) * LANES] = (
                zbuf_ref[slab, pad + tile - (CONV_W - 1):pad + tile, :])

    zbuf_ref[:, 0:pad, :] = zbuf_ref[:, tile:tile + pad, :]


def _prompt_ffn(x1, g_ffn, w_up, conv_w, conv_b, w_down, g_final):
    nb, seq, _ = x1.shape
    tile = PROMPT_TILE
    row_spec = pl.BlockSpec((None, tile, D_MODEL), lambda b, s: (b, s, 0))
    return pl.pallas_call(
        functools.partial(_prompt_ffn_kernel, tile=tile),
        grid=(nb, seq // tile),
        in_specs=[
            row_spec,
            _const_spec((1, D_MODEL)),
            _const_spec((D_MODEL, 2 * D_FF)),
            _const_spec((CONV_W, 2 * D_FF)),
            _const_spec((1, 2 * D_FF)),
            _const_spec((D_FF, D_MODEL)),
            _const_spec((1, D_MODEL)),
        ],
        out_specs=[row_spec, pl.BlockSpec((None, CONV_W - 1, 2 * D_FF), lambda b, s: (b, 0, 0))],
        out_shape=[
            jax.ShapeDtypeStruct((nb, seq, D_MODEL), F32),
            jax.ShapeDtypeStruct((nb, CONV_W - 1, 2 * D_FF), F32),
        ],
        scratch_shapes=[
            pltpu.VMEM((2 * D_FF // LANES, tile + SUBLANES, LANES), F32),
            pltpu.VMEM((tile, D_FF), BF16),
            pltpu.VMEM((D_MODEL // LANES, tile, LANES), F32),
            pltpu.VMEM((tile, D_MODEL), BF16),
        ],
        compiler_params=pltpu.CompilerParams(
            dimension_semantics=("arbitrary", "arbitrary"), vmem_limit_bytes=VMEM_LIMIT),
        name="prompt_ffn",
    )(x1, g_ffn, w_up, conv_w, conv_b, w_down, g_final)


def _sample_mixer_kernel(sinks_ref, x_ref, ck_ref, cv_ref, inv_ref, g_ref, win_ref, lng_ref, lnb_ref,
                         ws0_ref, bs0_ref, wpa_ref, wpb_ref, wo_ref,
                         x1_ref, gm_ref, ckn_ref, cvn_ref,
                         qz_ref, kn_ref, vn_ref, o_ref, *, chunk):
    lane = lax.broadcasted_iota(jnp.int32, (chunk, LANES), 1)
    low = lane < HEAD_DIM
    first_half = (lane % HEAD_DIM) < HEAD_DIM // 2

    x = x_ref[...]
    h = _rms(x, g_ref[...]).astype(BF16)
    a = _gelu(_dot(h, win_ref[:, 0:OFF_Q]))
    u = a[:, 0:GM_WIDTH]
    vg = _layer_norm(a[:, GM_WIDTH:OFF_Q], lng_ref[...], lnb_ref[...])
    gm_ref[...] = vg
    ya = (u * (vg * ws0_ref[...] + bs0_ref[...])).astype(BF16)

    zqkv = _dot(h, win_ref[:, OFF_Q:OFF_GA])
    pos = jnp.full((chunk, LANES), float(PAST_LEN), F32)
    cos, sin_signed = _rope_tables(pos, inv_ref[...], (chunk, LANES))
    knew = _rope(zqkv[:, Q_WIDTH:Q_WIDTH + KV_WIDTH], cos, sin_signed, first_half)
    vnew = zqkv[:, Q_WIDTH + KV_WIDTH:Q_WIDTH + 2 * KV_WIDTH]
    kn_ref[...] = knew
    vn_ref[...] = vnew
    for j in range(Q_WIDTH // LANES):
        qc = _rope(zqkv[:, j * LANES:(j + 1) * LANES], cos, sin_signed, first_half) * (HEAD_DIM ** -0.5)
        qsw = pltpu.roll(qc, HEAD_DIM, 1)
        g = j // 2
        even = jnp.where(low, qc, 0.0) if g == 0 else jnp.where(low, 0.0, qsw)
        odd = jnp.where(low, qsw, 0.0) if g == 0 else jnp.where(low, 0.0, qc)
        qz_ref[(2 * j) * chunk:(2 * j + 1) * chunk, :] = even
        qz_ref[(2 * j + 1) * chunk:(2 * j + 2) * chunk, :] = odd

    hrow = lax.broadcasted_iota(jnp.int32, (N_HEADS, 1), 0)
    sink_col = jnp.zeros((N_HEADS, 1), F32)
    for hd in range(N_HEADS):
        sink_col = jnp.where(hrow == hd, sinks_ref[hd], sink_col)
    head_low = lax.broadcasted_iota(jnp.int32, (N_HEADS, LANES), 0) < N_HEADS // KV_HEADS
    lane_low = lax.broadcasted_iota(jnp.int32, (N_HEADS, LANES), 1) < HEAD_DIM
    own_half = head_low == lane_low
    krow = lax.broadcasted_iota(jnp.int32, (CHUNK, LANES), 0)

    def body(bi, carry):
        qz = qz_ref[pl.ds(bi, N_HEADS, stride=chunk), :]
        kb = ck_ref[bi]
        vb = cv_ref[bi]
        kn = kn_ref[pl.ds(bi, 1), :]
        vn = vn_ref[pl.ds(bi, 1), :]
        sc = _dot_nt(qz.astype(BF16), kb.astype(BF16))
        sn = jnp.sum(qz * kn, axis=-1, keepdims=True)
        m = jnp.maximum(jnp.maximum(jnp.max(sc, axis=-1, keepdims=True), sn), sink_col)
        p = jnp.exp(sc - m)
        pn = jnp.exp(sn - m)
        lsum = jnp.sum(p, axis=-1, keepdims=True) + pn + jnp.exp(sink_col - m)
        o = (_dot(p.astype(BF16), vb.astype(BF16)) + pn * vn) / lsum
        o_ref[pl.ds(bi, N_HEADS, stride=chunk), :] = jnp.where(own_half, o, 0.0)
        ckn_ref[bi] = jnp.where(krow == CHUNK - 1, kn, pltpu.roll(kb, CHUNK - 1, 0))
        cvn_ref[bi] = jnp.where(krow == CHUNK - 1, vn, pltpu.roll(vb, CHUNK - 1, 0))
        return carry

    lax.fori_loop(0, chunk, body, 0, unroll=SAMPLE_UNROLL)

    cols = []
    for j in range(Q_WIDTH // LANES):
        g = j // 2
        oe = o_ref[(2 * j) * chunk:(2 * j + 1) * chunk, :]
        oo = o_ref[(2 * j + 1) * chunk:(2 * j + 2) * chunk, :]
        if g == 0:
            cols.append(jnp.where(low, oe, pltpu.roll(oo, HEAD_DIM, 1)))
        else:
            cols.append(jnp.where(low, pltpu.roll(oe, HEAD_DIM, 1), oo))
    yb = jnp.concatenate(cols, axis=1).astype(BF16)

    gates = _dot(h, win_ref[:, OFF_GA:IN_WIDTH])
    merged = (_sigmoid(gates[:, 0:D_MODEL]) * _dot(ya, wpa_ref[...])
              + _sigmoid(gates[:, D_MODEL:2 * D_MODEL]) * _dot(yb, wpb_ref[...]))
    x1_ref[...] = x + _dot(merged.astype(BF16), wo_ref[...])


def _sample_mixer(x, ck, cv, sinks, inv_row, g_mix, w_in, ln_g, ln_b, ws0, bs0, w_pa, w_pb, w_o):
    n = x.shape[0]
    chunk = SAMPLE_CHUNK
    row = lambda w: pl.BlockSpec((chunk, w), lambda i: (i, 0))
    cache = pl.BlockSpec((chunk, CHUNK, KV_WIDTH), lambda i: (i, 0, 0))
    return pl.pallas_call(
        functools.partial(_sample_mixer_kernel, chunk=chunk),
        grid=(n // chunk,),
        in_specs=[
            pl.BlockSpec(memory_space=pltpu.SMEM),
            row(D_MODEL), cache, cache,
            _const_spec((1, LANES)),
            _const_spec((1, D_MODEL)),
            _const_spec((D_MODEL, IN_WIDTH)),
            _const_spec((1, GM_WIDTH)),
            _const_spec((1, GM_WIDTH)),
            _const_spec((1, GM_WIDTH)),
            _const_spec((1, GM_WIDTH)),
            _const_spec((GM_WIDTH, D_MODEL)),
            _const_spec((Q_WIDTH, D_MODEL)),
            _const_spec((D_MODEL, D_MODEL)),
        ],
        out_specs=[row(D_MODEL), row(GM_WIDTH), cache, cache],
        out_shape=[
            jax.ShapeDtypeStruct((n, D_MODEL), F32),
            jax.ShapeDtypeStruct((n, GM_WIDTH), F32),
            jax.ShapeDtypeStruct((n, CHUNK, KV_WIDTH), F32),
            jax.ShapeDtypeStruct((n, CHUNK, KV_WIDTH), F32),
        ],
        scratch_shapes=[
            pltpu.VMEM((N_HEADS * chunk, LANES), F32),
            pltpu.VMEM((chunk, KV_WIDTH), F32),
            pltpu.VMEM((chunk, KV_WIDTH), F32),
            pltpu.VMEM((N_HEADS * chunk, LANES), F32),
        ],
        compiler_params=pltpu.CompilerParams(
            dimension_semantics=("arbitrary",), vmem_limit_bytes=VMEM_LIMIT),
        name="sample_mixer",
    )(sinks, x, ck, cv, inv_row, g_mix, w_in, ln_g, ln_b, ws0, bs0, w_pa, w_pb, w_o)


def _sample_ffn_kernel(x_ref, g_ref, wua_ref, wub_ref, cwa_ref, cwb_ref, cba_ref, cbb_ref,
                       p0a_ref, p0b_ref, p1a_ref, p1b_ref, wdn_ref, gf_ref,
                       y_ref, za_ref, zb_ref, acc_ref):
    j = pl.program_id(0)
    x = x_ref[...]
    h = _rms(x, g_ref[...]).astype(BF16)
    za = _dot(h, wua_ref[...])
    zb = _dot(h, wub_ref[...])
    za_ref[...] = za
    zb_ref[...] = zb
    ca = cba_ref[...] + cwa_ref[0:1, :] * p0a_ref[...] + cwa_ref[1:2, :] * p1a_ref[...] + cwa_ref[2:3, :] * za
    cb = cbb_ref[...] + cwb_ref[0:1, :] * p0b_ref[...] + cwb_ref[1:2, :] * p1b_ref[...] + cwb_ref[2:3, :] * zb
    part = _dot((_gelu(ca) * cb).astype(BF16), wdn_ref[...])

    @pl.when(j == 0)
    def _first():
        acc_ref[...] = part

    @pl.when(j > 0)
    def _rest():
        acc_ref[...] += part

    @pl.when(j == pl.num_programs(0) - 1)
    def _finish():
        y_ref[...] = _rms(x + acc_ref[...], gf_ref[...])


def _sample_ffn(x1, state2d, g_ffn, w_up, conv_w, conv_b, w_down, g_final):
    n = x1.shape[0]
    nchunk = D_FF // FF_CHUNK
    full = lambda w: pl.BlockSpec((n, w), lambda j: (0, 0))
    col = lambda rows, off: pl.BlockSpec((rows, FF_CHUNK), lambda j, off=off: (0, j + off))
    return pl.pallas_call(
        _sample_ffn_kernel,
        grid=(nchunk,),
        in_specs=[
            full(D_MODEL),
            pl.BlockSpec((1, D_MODEL), lambda j: (0, 0)),
            col(D_MODEL, 0), col(D_MODEL, nchunk),
            col(CONV_W, 0), col(CONV_W, nchunk),
            col(1, 0), col(1, nchunk),
            col(n, 0), col(n, nchunk), col(n, 2 * nchunk), col(n, 3 * nchunk),
            pl.BlockSpec((FF_CHUNK, D_MODEL), lambda j: (j, 0)),
            pl.BlockSpec((1, D_MODEL), lambda j: (0, 0)),
        ],
        out_specs=[full(D_MODEL), col(n, 0), col(n, 0)],
        out_shape=[
            jax.ShapeDtypeStruct((n, D_MODEL), F32),
            jax.ShapeDtypeStruct((n, D_FF), F32),
            jax.ShapeDtypeStruct((n, D_FF), F32),
        ],
        scratch_shapes=[pltpu.VMEM((n, D_MODEL), F32)],
        compiler_params=pltpu.CompilerParams(
            dimension_semantics=("arbitrary",), vmem_limit_bytes=VMEM_LIMIT),
        name="sample_ffn",
    )(x1, g_ffn, w_up, w_up, conv_w, conv_w, conv_b, conv_b,
      state2d, state2d, state2d, state2d, w_down, g_final)


def kernel(x_prompt, x_sample, cache_swa_k, cache_swa_v, state_ffn_conv, g_mix, w_in, ln_v_g, ln_v_b,
           w_s, b_s, sinks, w_pa, w_pb, w_o, g_ffn, w_up, conv_w, conv_b, w_down, g_final):
    depth = g_mix.shape[0]
    assert depth == 1
    nb = x_prompt.shape[0]
    nd = x_sample.shape[0]
    half = HEAD_DIM // 2
    inv = ROPE_THETA ** (-jnp.arange(half, dtype=F32) / half)
    inv_row = jnp.tile(inv, LANES // half)[None, :]

    l = 0
    row = lambda v: v[l][None, :]
    w_in_b = w_in[l].astype(BF16)
    w_pa_b = w_pa[l].astype(BF16)
    w_pb_b = w_pb[l].astype(BF16)
    w_o_b = w_o[l].astype(BF16)
    w_up_b = w_up[l].astype(BF16)
    w_dn_b = w_down[l].astype(BF16)
    bs_full = jnp.repeat(b_s[l].T, HEAD_DIM, axis=1)
    ws0 = jnp.repeat(w_s[l][:, 0, 0], HEAD_DIM)[None, :]
    bs0 = bs_full[0:1]

    x1p, kp, vp, gmp = _prompt_mixer(x_prompt, sinks[l], inv_row, row(g_mix), w_in_b, row(ln_v_g), row(ln_v_b),
                                     w_s[l], bs_full, w_pa_b, w_pb_b, w_o_b)
    yp, convp = _prompt_ffn(x1p, row(g_ffn), w_up_b, conv_w[l], row(conv_b), w_dn_b, g_final[None, :])

    xs = x_sample.reshape(nd, D_MODEL)
    ck = cache_swa_k[l].reshape(nd, CHUNK, KV_WIDTH)
    cv = cache_swa_v[l].reshape(nd, CHUNK, KV_WIDTH)
    x1s, gms, ckn, cvn = _sample_mixer(xs, ck, cv, sinks[l], inv_row, row(g_mix), w_in_b, row(ln_v_g),
                                       row(ln_v_b), ws0, bs0, w_pa_b, w_pb_b, w_o_b)
    state = state_ffn_conv[l]
    ys, za, zb = _sample_ffn(x1s, state.reshape(nd, (CONV_W - 1) * 2 * D_FF), row(g_ffn), w_up_b,
                             conv_w[l], row(conv_b), w_dn_b, g_final[None, :])
    conv_s = jnp.stack([state[:, 1, :], jnp.concatenate([za, zb], axis=1)], axis=1)

    return (yp,
            ys.reshape(nd, 1, D_MODEL),
            kp.reshape(1, nb, CHUNK, KV_HEADS, HEAD_DIM),
            vp.reshape(1, nb, CHUNK, KV_HEADS, HEAD_DIM),
            gmp.reshape(1, nb, CHUNK, GM_HEADS, GM_WIDTH // GM_HEADS),
            convp[None],
            ckn.reshape(1, nd, CHUNK, KV_HEADS, HEAD_DIM),
            cvn.reshape(1, nd, CHUNK, KV_HEADS, HEAD_DIM),
            gms.reshape(1, nd, 1, GM_HEADS, GM_WIDTH // GM_HEADS),
            conv_s[None])
```

```python
import functools
import math

import jax
import jax.numpy as jnp
from jax import lax
from jax.experimental import pallas as pl
from jax.experimental.pallas import tpu as pltpu

F32 = jnp.float32
BF16 = jnp.bfloat16

D_MODEL = 1024
SEQ = 2048
PAST_LEN = 16384
CHUNK = 128
GM_HEADS = 8
GM_WIDTH = 512
N_HEADS = 8
KV_HEADS = 2
HEAD_DIM = 64
Q_WIDTH = N_HEADS * HEAD_DIM
KV_WIDTH = KV_HEADS * HEAD_DIM
D_FF = 3 * D_MODEL
CONV_W = 3
EPS = 1e-6
ROPE_THETA = 10000.0
LANES = 128
SUBLANES = 8
OFF_Q = 2 * GM_WIDTH
OFF_K = OFF_Q + Q_WIDTH
OFF_V = OFF_K + KV_WIDTH
OFF_GA = OFF_V + KV_WIDTH
OFF_GB = OFF_GA + D_MODEL
IN_WIDTH = OFF_GB + D_MODEL

PROMPT_TILE = 512
SAMPLE_CHUNK = 16
SAMPLE_UNROLL = 8
FF_CHUNK = 512
FFN_LAG = 4
ROW_PHASES = 4
VMEM_LIMIT = 52 * 1024 * 1024

_GELU_C0 = math.sqrt(2.0 / math.pi)
_GELU_C1 = _GELU_C0 * 0.044715


def _gelu(x):
    inner = x * (_GELU_C0 + _GELU_C1 * (x * x))
    hx = 0.5 * x
    return hx + hx * jnp.tanh(inner)


def _rms(x, g):
    ms = jnp.mean(x * x, axis=-1, keepdims=True)
    return x * lax.rsqrt(ms + EPS) * g


def _layer_norm(x, g, b):
    mu = jnp.mean(x, axis=-1, keepdims=True)
    xc = x - mu
    var = jnp.mean(xc * xc, axis=-1, keepdims=True)
    return xc * lax.rsqrt(var + EPS) * g + b


def _sigmoid(x):
    return 1.0 / (1.0 + jnp.exp(-x))


def _dot(a, b):
    return jnp.dot(a, b, preferred_element_type=F32)


def _dot_nt(a, b):
    return lax.dot_general(a, b, (((1,), (1,)), ((), ())), preferred_element_type=F32)


def _rope(xc, cos, sin_signed, first_half):
    partner = jnp.where(first_half, pltpu.roll(xc, LANES - 32, 1), pltpu.roll(xc, 32, 1))
    return xc * cos + partner * sin_signed


def _rope_tables(pos_f32, inv_row, shape):
    lane = lax.broadcasted_iota(jnp.int32, shape, 1)
    ang = pos_f32 * inv_row
    sin = jnp.sin(ang)
    return jnp.cos(ang), jnp.where((lane % HEAD_DIM) < HEAD_DIM // 2, -sin, sin)


def _prompt_mixer_kernel(sinks_ref, x_ref, inv_ref, g_ref, win_ref, lng_ref, lnb_ref, ws_ref, bs_ref,
                         wpa_ref, wpb_ref, wo_ref,
                         x1_ref, klast_ref, vlast_ref, gmlast_ref,
                         cos_ref, sin_ref, wcat_ref, kprev_ref, vprev_ref, h_ref, *, tile):
    b = pl.program_id(0)
    s = pl.program_id(1)
    nblk = tile // CHUNK

    @pl.when((b == 0) & (s == 0))
    def _init_tables():
        def body(i, carry):
            r0 = pl.multiple_of(i * CHUNK, CHUNK)
            pos = (lax.broadcasted_iota(jnp.int32, (CHUNK, LANES), 0) + r0).astype(F32)
            cos, sin_signed = _rope_tables(pos, inv_ref[...], (CHUNK, LANES))
            cos_ref[pl.ds(r0, CHUNK), :] = cos
            sin_ref[pl.ds(r0, CHUNK), :] = sin_signed
            return carry
        lax.fori_loop(0, SEQ // CHUNK, body, 0)
        row = lax.broadcasted_iota(jnp.int32, (CHUNK, CHUNK), 0)
        col = lax.broadcasted_iota(jnp.int32, (CHUNK, CHUNK), 1)
        causal = col <= row
        for j in range(GM_HEADS // 2):
            wcat_ref[j, :, 0:CHUNK] = jnp.where(causal, ws_ref[2 * j], 0.0).astype(BF16)
            wcat_ref[j, :, CHUNK:2 * CHUNK] = jnp.where(causal, ws_ref[2 * j + 1], 0.0).astype(BF16)

    @pl.when(s == 0)
    def _reset_carry():
        kprev_ref[...] = jnp.zeros_like(kprev_ref)
        vprev_ref[...] = jnp.zeros_like(vprev_ref)

    lane = lax.broadcasted_iota(jnp.int32, (CHUNK, LANES), 1)
    low = lane < HEAD_DIM
    low2 = lax.broadcasted_iota(jnp.int32, (2 * CHUNK, LANES), 1) < HEAD_DIM
    first_half_t = (lax.broadcasted_iota(jnp.int32, (tile, LANES), 1) % HEAD_DIM) < HEAD_DIM // 2
    row = lax.broadcasted_iota(jnp.int32, (CHUNK, CHUNK), 0)
    col = lax.broadcasted_iota(jnp.int32, (CHUNK, CHUNK), 1)
    cur_mask = col <= row
    prev_live = (col - row) >= jnp.where(s > 0, 0, 2 * CHUNK)
    prev_inner = col >= row

    x = x_ref[...]
    h_ref[...] = _rms(x, g_ref[...]).astype(BF16)

    zqkv = _dot(h_ref[...], win_ref[:, OFF_Q:OFF_GA])
    zuv = _dot(h_ref[...], win_ref[:, 0:OFF_Q])

    r0 = pl.multiple_of(s * tile, tile)
    cos = cos_ref[pl.ds(r0, tile), :]
    sin_signed = sin_ref[pl.ds(r0, tile), :]
    qcols = [(_rope(zqkv[:, j * LANES:(j + 1) * LANES], cos, sin_signed, first_half_t)
              * (HEAD_DIM ** -0.5)).astype(BF16) for j in range(Q_WIDTH // LANES)]
    krot = _rope(zqkv[:, Q_WIDTH:Q_WIDTH + KV_WIDTH], cos, sin_signed, first_half_t)
    vval = zqkv[:, Q_WIDTH + KV_WIDTH:Q_WIDTH + 2 * KV_WIDTH]

    kp = kprev_ref[...]
    vp = vprev_ref[...]
    scores = {}
    vblks = {}
    for i in range(nblk):
        r = slice(i * CHUNK, (i + 1) * CHUNK)
        kk = jnp.concatenate([kp, krot[r]], axis=0)
        vv = jnp.concatenate([vp, vval[r]], axis=0)
        kk_sw = pltpu.roll(kk, HEAD_DIM, 1)
        vv_sw = pltpu.roll(vv, HEAD_DIM, 1)
        for g in range(KV_HEADS):
            k_lo = jnp.where(low2, kk if g == 0 else kk_sw, 0.0).astype(BF16)
            k_hi = jnp.where(low2, 0.0, kk_sw if g == 0 else kk).astype(BF16)
            v_lo = jnp.where(low2, vv if g == 0 else vv_sw, 0.0).astype(BF16)
            v_hi = jnp.where(low2, 0.0, vv_sw if g == 0 else vv).astype(BF16)
            krhs = jnp.concatenate([k_lo, k_hi], axis=0)
            vblks[i, g] = jnp.concatenate([v_lo, v_hi], axis=0)
            qstack = jnp.concatenate([qcols[2 * g][r], qcols[2 * g + 1][r]], axis=0)
            scores[i, g] = _dot_nt(qstack, krhs)
        kp = krot[r]
        vp = vval[r]
    kprev_ref[...] = kp
    vprev_ref[...] = vp

    gates = _dot(h_ref[...], win_ref[:, OFF_GA:IN_WIDTH])

    a = _gelu(zuv)
    u = a[:, 0:GM_WIDTH]
    vg = _layer_norm(a[:, GM_WIDTH:OFF_Q], lng_ref[...], lnb_ref[...])
    ya_blocks = []
    for c in range(nblk):
        r = slice(c * CHUNK, (c + 1) * CHUNK)
        cols = []
        for j in range(GM_HEADS // 2):
            vcol = vg[r, j * LANES:(j + 1) * LANES]
            rhs = jnp.concatenate([jnp.where(low, vcol, 0.0), jnp.where(low, 0.0, vcol)], axis=0).astype(BF16)
            cols.append(_dot(wcat_ref[j], rhs))
        mix = jnp.concatenate(cols, axis=1) + bs_ref[...]
        ya_blocks.append((u[r] * mix).astype(BF16))
    ya = jnp.concatenate(ya_blocks, axis=0)
    pa = _dot(ya, wpa_ref[...])

    yb_blocks = []
    for i in range(nblk):
        pmask = prev_live if i == 0 else prev_inner
        cols_out = []
        for g in range(KV_HEADS):
            sc = scores[i, g]
            p_rows = []
            rinv_rows = []
            for jj in range(2):
                ps = []
                ls = []
                for par in range(2):
                    hd = 4 * g + 2 * jj + par
                    sh = sc[jj * CHUNK:(jj + 1) * CHUNK, par * 2 * CHUNK:(par + 1) * 2 * CHUNK]
                    sp = jnp.where(pmask, sh[:, 0:CHUNK], -jnp.inf)
                    scur = jnp.where(cur_mask, sh[:, CHUNK:2 * CHUNK], -jnp.inf)
                    sink = sinks_ref[hd]
                    m = jnp.maximum(jnp.max(jnp.maximum(sp, scur), axis=-1, keepdims=True), sink)
                    pp = jnp.exp(sp - m)
                    pc = jnp.exp(scur - m)
                    lsum = jnp.sum(pp + pc, axis=-1, keepdims=True) + jnp.exp(sink - m)
                    ps.append(pp.astype(BF16))
                    ps.append(pc.astype(BF16))
                    ls.append(lsum)
                p_rows.append(jnp.concatenate(ps, axis=1))
                rinv_rows.append(jnp.where(low, 1.0 / ls[0], 1.0 / ls[1]))
            o = _dot(jnp.concatenate(p_rows, axis=0), vblks[i, g])
            cols_out.append(o[0:CHUNK] * rinv_rows[0])
            cols_out.append(o[CHUNK:2 * CHUNK] * rinv_rows[1])
        yb_blocks.append(jnp.concatenate(cols_out, axis=1).astype(BF16))
    yb = jnp.concatenate(yb_blocks, axis=0)

    merged = (_sigmoid(gates[:, 0:D_MODEL]) * pa
              + _sigmoid(gates[:, D_MODEL:2 * D_MODEL]) * _dot(yb, wpb_ref[...]))
    x1_ref[...] = x + _dot(merged.astype(BF16), wo_ref[...])

    @pl.when(s == pl.num_programs(1) - 1)
    def _emit_state():
        klast_ref[...] = krot[tile - CHUNK:tile]
        vlast_ref[...] = vval[tile - CHUNK:tile]
        gmlast_ref[...] = vg[tile - CHUNK:tile]


def _const_spec(shape):
    nd = len(shape)
    return pl.BlockSpec(shape, lambda *_: (0,) * nd, pipeline_mode=pl.Buffered(1))


def _prompt_mixer(x, sinks, inv_row, g_mix, w_in, ln_g, ln_b, w_s, bs_full, w_pa, w_pb, w_o):
    nb, seq, _ = x.shape
    tile = PROMPT_TILE
    grid = (nb, seq // tile)
    row_spec = pl.BlockSpec((None, tile, D_MODEL), lambda b, s: (b, s, 0))
    last = lambda w: pl.BlockSpec((None, CHUNK, w), lambda b, s: (b, 0, 0))
    return pl.pallas_call(
        functools.partial(_prompt_mixer_kernel, tile=tile),
        grid=grid,
        in_specs=[
            pl.BlockSpec(memory_space=pltpu.SMEM),
            row_spec,
            _const_spec((1, LANES)),
            _const_spec((1, D_MODEL)),
            _const_spec((D_MODEL, IN_WIDTH)),
            _const_spec((1, GM_WIDTH)),
            _const_spec((1, GM_WIDTH)),
            _const_spec((GM_HEADS, CHUNK, CHUNK)),
            _const_spec((CHUNK, GM_WIDTH)),
            _const_spec((GM_WIDTH, D_MODEL)),
            _const_spec((Q_WIDTH, D_MODEL)),
            _const_spec((D_MODEL, D_MODEL)),
        ],
        out_specs=[row_spec, last(KV_WIDTH), last(KV_WIDTH), last(GM_WIDTH)],
        out_shape=[
            jax.ShapeDtypeStruct((nb, seq, D_MODEL), F32),
            jax.ShapeDtypeStruct((nb, CHUNK, KV_WIDTH), F32),
            jax.ShapeDtypeStruct((nb, CHUNK, KV_WIDTH), F32),
            jax.ShapeDtypeStruct((nb, CHUNK, GM_WIDTH), F32),
        ],
        scratch_shapes=[
            pltpu.VMEM((SEQ, LANES), F32),
            pltpu.VMEM((SEQ, LANES), F32),
            pltpu.VMEM((GM_HEADS // 2, CHUNK, 2 * CHUNK), BF16),
            pltpu.VMEM((CHUNK, KV_WIDTH), F32),
            pltpu.VMEM((CHUNK, KV_WIDTH), F32),
            pltpu.VMEM((tile, D_MODEL), BF16),
        ],
        compiler_params=pltpu.CompilerParams(
            dimension_semantics=("arbitrary", "arbitrary"), vmem_limit_bytes=VMEM_LIMIT),
        name="prompt_mixer",
    )(sinks, x, inv_row, g_mix, w_in, ln_g, ln_b, w_s, bs_full, w_pa, w_pb, w_o)


def _prompt_ffn_kernel(*refs, tile):
    xs_refs = refs[:D_MODEL // LANES]
    (g_ref, wup_ref, cw_ref, cb_ref, wdn_ref, gf_ref,
     y_ref, convlast_ref, zbuf_ref, gbuf_ref, perm_ref, h_ref) = refs[D_MODEL // LANES:]
    s = pl.program_id(1)
    q = tile // ROW_PHASES
    nslab = D_MODEL // LANES
    nslab_half = D_FF // LANES
    off = [SUBLANES * (r + 1) + r * q for r in range(ROW_PHASES)]
    wrapped = (ROW_PHASES - 2, ROW_PHASES - 1)

    @pl.when(s == 0)
    def _reset_carry():
        for r in wrapped:
            zbuf_ref[:, off[r] - SUBLANES:off[r], :] = jnp.zeros((2 * nslab_half, SUBLANES, LANES), F32)

    def x_phase_major(j):
        return jnp.concatenate([xs_refs[j][pl.ds(r, q, stride=ROW_PHASES), :] for r in range(ROW_PHASES)],
                               axis=0)

    def rms_slabs(slabs, gain_ref):
        ms = jnp.sum(sum(v * v for v in slabs), axis=-1, keepdims=True) * (1.0 / D_MODEL)
        rinv = lax.rsqrt(ms + EPS)
        return [v * rinv * gain_ref[:, j * LANES:(j + 1) * LANES] for j, v in enumerate(slabs)]

    for j, v in enumerate(rms_slabs([x_phase_major(j) for j in range(nslab)], g_ref)):
        h_ref[:, j * LANES:(j + 1) * LANES] = v.astype(BF16)

    def up_project(j):
        for half in range(2):
            n0 = half * D_FF + j * FF_CHUNK
            z = _dot(h_ref[...], wup_ref[:, n0:n0 + FF_CHUNK])
            for jj in range(FF_CHUNK // LANES):
                for r in range(ROW_PHASES):
                    zbuf_ref[n0 // LANES + jj, off[r]:off[r] + q, :] = z[r * q:(r + 1) * q, jj * LANES:(jj + 1) * LANES]

    def conv(slab):
        cs = slice(slab * LANES, (slab + 1) * LANES)
        blocks = [zbuf_ref[slab, off[r]:off[r] + q, :] for r in range(ROW_PHASES)]
        moved = {r: zbuf_ref[slab, off[r] - 1:off[r] - 1 + q, :] for r in wrapped}
        taps = [cw_ref[t:t + 1, cs] for t in range(CONV_W)]
        bias = cb_ref[:, cs]
        out = []
        for r in range(ROW_PHASES):
            two_back = blocks[r - 2] if r >= 2 else moved[r + ROW_PHASES - 2]
            one_back = blocks[r - 1] if r >= 1 else moved[ROW_PHASES - 1]
            out.append(bias + taps[0] * two_back + taps[1] * one_back + taps[2] * blocks[r])
        return out

    def gate_and_project(j, yperm):
        for ja in range(j * (FF_CHUNK // LANES), (j + 1) * (FF_CHUNK // LANES)):
            ca = conv(ja)
            cb = conv(ja + nslab_half)
            for r in range(ROW_PHASES):
                gbuf_ref[r * q:(r + 1) * q, ja * LANES:(ja + 1) * LANES] = (_gelu(ca[r]) * cb[r]).astype(BF16)
        cs = slice(j * FF_CHUNK, (j + 1) * FF_CHUNK)
        part = _dot(gbuf_ref[:, cs], wdn_ref[cs, :])
        return part if yperm is None else yperm + part

    nchunk = D_FF // FF_CHUNK
    yperm = None
    for j in range(nchunk + FFN_LAG):
        if j < nchunk:
            up_project(j)
        if j >= FFN_LAG:
            yperm = gate_and_project(j - FFN_LAG, yperm)

    x2 = [x_phase_major(j) + yperm[:, j * LANES:(j + 1) * LANES] for j in range(nslab)]
    for j, v in enumerate(rms_slabs(x2, gf_ref)):
        for r in range(ROW_PHASES):
            perm_ref[j, pl.ds(r, q, stride=ROW_PHASES), :] = v[r * q:(r + 1) * q]
    for j in range(nslab):
        y_ref[:, j * LANES:(j + 1) * LANES] = perm_ref[j]

    @pl.when(s == pl.num_programs(1) - 1)
    def _emit_state():
        for slab in range(2 * nslab_half):
            for i, r in enumerate(wrapped):
                convlast_ref[i:i + 1, slab * LANES:(slab + 1) * LANES] = (
                    zbuf_ref[slab, off[r] + q - 1:off[r] + q, :])

    for r in wrapped:
        zbuf_ref[:, off[r] - SUBLANES:off[r], :] = zbuf_ref[:, off[r] + q - SUBLANES:off[r] + q, :]


def _prompt_ffn(x1, g_ffn, w_up, conv_w, conv_b, w_down, g_final):
    nb, seq, _ = x1.shape
    tile = PROMPT_TILE
    nslab = D_MODEL // LANES
    row_spec = pl.BlockSpec((None, tile, D_MODEL), lambda b, s: (b, s, 0))
    return pl.pallas_call(
        functools.partial(_prompt_ffn_kernel, tile=tile),
        grid=(nb, seq // tile),
        in_specs=[pl.BlockSpec((None, tile, LANES), lambda b, s, j=j: (b, s, j)) for j in range(nslab)] + [
            _const_spec((1, D_MODEL)),
            _const_spec((D_MODEL, 2 * D_FF)),
            _const_spec((CONV_W, 2 * D_FF)),
            _const_spec((1, 2 * D_FF)),
            _const_spec((D_FF, D_MODEL)),
            _const_spec((1, D_MODEL)),
        ],
        out_specs=[row_spec, pl.BlockSpec((None, CONV_W - 1, 2 * D_FF), lambda b, s: (b, 0, 0))],
        out_shape=[
            jax.ShapeDtypeStruct((nb, seq, D_MODEL), F32),
            jax.ShapeDtypeStruct((nb, CONV_W - 1, 2 * D_FF), F32),
        ],
        scratch_shapes=[
            pltpu.VMEM((2 * D_FF // LANES, tile + ROW_PHASES * SUBLANES, LANES), F32),
            pltpu.VMEM((tile, D_FF), BF16),
            pltpu.VMEM((D_MODEL // LANES, tile, LANES), F32),
            pltpu.VMEM((tile, D_MODEL), BF16),
        ],
        compiler_params=pltpu.CompilerParams(
            dimension_semantics=("arbitrary", "arbitrary"), vmem_limit_bytes=VMEM_LIMIT),
        name="prompt_ffn",
    )(*([x1] * nslab), g_ffn, w_up, conv_w, conv_b, w_down, g_final)


def _sample_mixer_kernel(sinks_ref, x_ref, ck_ref, cv_ref, inv_ref, g_ref, win_ref, lng_ref, lnb_ref,
                         ws0_ref, bs0_ref, wpa_ref, wpb_ref, wo_ref,
                         x1_ref, gm_ref, ckn_ref, cvn_ref,
                         qz_ref, kn_ref, vn_ref, o_ref, *, chunk):
    lane = lax.broadcasted_iota(jnp.int32, (chunk, LANES), 1)
    low = lane < HEAD_DIM
    first_half = (lane % HEAD_DIM) < HEAD_DIM // 2

    x = x_ref[...]
    h = _rms(x, g_ref[...]).astype(BF16)
    a = _gelu(_dot(h, win_ref[:, 0:OFF_Q]))
    u = a[:, 0:GM_WIDTH]
    vg = _layer_norm(a[:, GM_WIDTH:OFF_Q], lng_ref[...], lnb_ref[...])
    gm_ref[...] = vg
    ya = (u * (vg * ws0_ref[...] + bs0_ref[...])).astype(BF16)

    zqkv = _dot(h, win_ref[:, OFF_Q:OFF_GA])
    pos = jnp.full((chunk, LANES), float(PAST_LEN), F32)
    cos, sin_signed = _rope_tables(pos, inv_ref[...], (chunk, LANES))
    knew = _rope(zqkv[:, Q_WIDTH:Q_WIDTH + KV_WIDTH], cos, sin_signed, first_half)
    vnew = zqkv[:, Q_WIDTH + KV_WIDTH:Q_WIDTH + 2 * KV_WIDTH]
    kn_ref[...] = knew
    vn_ref[...] = vnew
    for j in range(Q_WIDTH // LANES):
        qc = _rope(zqkv[:, j * LANES:(j + 1) * LANES], cos, sin_signed, first_half) * (HEAD_DIM ** -0.5)
        qsw = pltpu.roll(qc, HEAD_DIM, 1)
        g = j // 2
        even = jnp.where(low, qc, 0.0) if g == 0 else jnp.where(low, 0.0, qsw)
        odd = jnp.where(low, qsw, 0.0) if g == 0 else jnp.where(low, 0.0, qc)
        qz_ref[(2 * j) * chunk:(2 * j + 1) * chunk, :] = even
        qz_ref[(2 * j + 1) * chunk:(2 * j + 2) * chunk, :] = odd

    hrow = lax.broadcasted_iota(jnp.int32, (N_HEADS, 1), 0)
    sink_col = jnp.zeros((N_HEADS, 1), F32)
    for hd in range(N_HEADS):
        sink_col = jnp.where(hrow == hd, sinks_ref[hd], sink_col)
    head_low = lax.broadcasted_iota(jnp.int32, (N_HEADS, LANES), 0) < N_HEADS // KV_HEADS
    lane_low = lax.broadcasted_iota(jnp.int32, (N_HEADS, LANES), 1) < HEAD_DIM
    own_half = head_low == lane_low
    krow = lax.broadcasted_iota(jnp.int32, (CHUNK, LANES), 0)

    def body(bi, carry):
        qz = qz_ref[pl.ds(bi, N_HEADS, stride=chunk), :]
        kb = ck_ref[bi]
        vb = cv_ref[bi]
        kn = kn_ref[pl.ds(bi, 1), :]
        vn = vn_ref[pl.ds(bi, 1), :]
        sc = _dot_nt(qz.astype(BF16), kb.astype(BF16))
        sn = jnp.sum(qz * kn, axis=-1, keepdims=True)
        m = jnp.maximum(jnp.maximum(jnp.max(sc, axis=-1, keepdims=True), sn), sink_col)
        p = jnp.exp(sc - m)
        pn = jnp.exp(sn - m)
        lsum = jnp.sum(p, axis=-1, keepdims=True) + pn + jnp.exp(sink_col - m)
        o = (_dot(p.astype(BF16), vb.astype(BF16)) + pn * vn) / lsum
        o_ref[pl.ds(bi, N_HEADS, stride=chunk), :] = jnp.where(own_half, o, 0.0)
        ckn_ref[bi] = jnp.where(krow == CHUNK - 1, kn, pltpu.roll(kb, CHUNK - 1, 0))
        cvn_ref[bi] = jnp.where(krow == CHUNK - 1, vn, pltpu.roll(vb, CHUNK - 1, 0))
        return carry

    lax.fori_loop(0, chunk, body, 0, unroll=SAMPLE_UNROLL)

    cols = []
    for j in range(Q_WIDTH // LANES):
        g = j // 2
        oe = o_ref[(2 * j) * chunk:(2 * j + 1) * chunk, :]
        oo = o_ref[(2 * j + 1) * chunk:(2 * j + 2) * chunk, :]
        if g == 0:
            cols.append(jnp.where(low, oe, pltpu.roll(oo, HEAD_DIM, 1)))
        else:
            cols.append(jnp.where(low, pltpu.roll(oe, HEAD_DIM, 1), oo))
    yb = jnp.concatenate(cols, axis=1).astype(BF16)

    gates = _dot(h, win_ref[:, OFF_GA:IN_WIDTH])
    merged = (_sigmoid(gates[:, 0:D_MODEL]) * _dot(ya, wpa_ref[...])
              + _sigmoid(gates[:, D_MODEL:2 * D_MODEL]) * _dot(yb, wpb_ref[...]))
    x1_ref[...] = x + _dot(merged.astype(BF16), wo_ref[...])


def _sample_mixer(x, ck, cv, sinks, inv_row, g_mix, w_in, ln_g, ln_b, ws0, bs0, w_pa, w_pb, w_o):
    n = x.shape[0]
    chunk = SAMPLE_CHUNK
    row = lambda w: pl.BlockSpec((chunk, w), lambda i: (i, 0))
    cache = pl.BlockSpec((chunk, CHUNK, KV_WIDTH), lambda i: (i, 0, 0))
    return pl.pallas_call(
        functools.partial(_sample_mixer_kernel, chunk=chunk),
        grid=(n // chunk,),
        in_specs=[
            pl.BlockSpec(memory_space=pltpu.SMEM),
            row(D_MODEL), cache, cache,
            _const_spec((1, LANES)),
            _const_spec((1, D_MODEL)),
            _const_spec((D_MODEL, IN_WIDTH)),
            _const_spec((1, GM_WIDTH)),
            _const_spec((1, GM_WIDTH)),
            _const_spec((1, GM_WIDTH)),
            _const_spec((1, GM_WIDTH)),
            _const_spec((GM_WIDTH, D_MODEL)),
            _const_spec((Q_WIDTH, D_MODEL)),
            _const_spec((D_MODEL, D_MODEL)),
        ],
        out_specs=[row(D_MODEL), row(GM_WIDTH), cache, cache],
        out_shape=[
            jax.ShapeDtypeStruct((n, D_MODEL), F32),
            jax.ShapeDtypeStruct((n, GM_WIDTH), F32),
            jax.ShapeDtypeStruct((n, CHUNK, KV_WIDTH), F32),
            jax.ShapeDtypeStruct((n, CHUNK, KV_WIDTH), F32),
        ],
        scratch_shapes=[
            pltpu.VMEM((N_HEADS * chunk, LANES), F32),
            pltpu.VMEM((chunk, KV_WIDTH), F32),
            pltpu.VMEM((chunk, KV_WIDTH), F32),
            pltpu.VMEM((N_HEADS * chunk, LANES), F32),
        ],
        compiler_params=pltpu.CompilerParams(
            dimension_semantics=("arbitrary",), vmem_limit_bytes=VMEM_LIMIT),
        name="sample_mixer",
    )(sinks, x, ck, cv, inv_row, g_mix, w_in, ln_g, ln_b, ws0, bs0, w_pa, w_pb, w_o)


def _sample_ffn_kernel(x_ref, g_ref, wua_ref, wub_ref, cwa_ref, cwb_ref, cba_ref, cbb_ref,
                       p0a_ref, p0b_ref, p1a_ref, p1b_ref, wdn_ref, gf_ref,
                       y_ref, za_ref, zb_ref, acc_ref):
    j = pl.program_id(0)
    x = x_ref[...]
    h = _rms(x, g_ref[...]).astype(BF16)
    za = _dot(h, wua_ref[...])
    zb = _dot(h, wub_ref[...])
    za_ref[...] = za
    zb_ref[...] = zb
    ca = cba_ref[...] + cwa_ref[0:1, :] * p0a_ref[...] + cwa_ref[1:2, :] * p1a_ref[...] + cwa_ref[2:3, :] * za
    cb = cbb_ref[...] + cwb_ref[0:1, :] * p0b_ref[...] + cwb_ref[1:2, :] * p1b_ref[...] + cwb_ref[2:3, :] * zb
    part = _dot((_gelu(ca) * cb).astype(BF16), wdn_ref[...])

    @pl.when(j == 0)
    def _first():
        acc_ref[...] = part

    @pl.when(j > 0)
    def _rest():
        acc_ref[...] += part

    @pl.when(j == pl.num_programs(0) - 1)
    def _finish():
        y_ref[...] = _rms(x + acc_ref[...], gf_ref[...])


def _sample_ffn(x1, state2d, g_ffn, w_up, conv_w, conv_b, w_down, g_final):
    n = x1.shape[0]
    nchunk = D_FF // FF_CHUNK
    full = lambda w: pl.BlockSpec((n, w), lambda j: (0, 0))
    col = lambda rows, off: pl.BlockSpec((rows, FF_CHUNK), lambda j, off=off: (0, j + off))
    return pl.pallas_call(
        _sample_ffn_kernel,
        grid=(nchunk,),
        in_specs=[
            full(D_MODEL),
            pl.BlockSpec((1, D_MODEL), lambda j: (0, 0)),
            col(D_MODEL, 0), col(D_MODEL, nchunk),
            col(CONV_W, 0), col(CONV_W, nchunk),
            col(1, 0), col(1, nchunk),
            col(n, 0), col(n, nchunk), col(n, 2 * nchunk), col(n, 3 * nchunk),
            pl.BlockSpec((FF_CHUNK, D_MODEL), lambda j: (j, 0)),
            pl.BlockSpec((1, D_MODEL), lambda j: (0, 0)),
        ],
        out_specs=[full(D_MODEL), col(n, 0), col(n, 0)],
        out_shape=[
            jax.ShapeDtypeStruct((n, D_MODEL), F32),
            jax.ShapeDtypeStruct((n, D_FF), F32),
            jax.ShapeDtypeStruct((n, D_FF), F32),
        ],
        scratch_shapes=[pltpu.VMEM((n, D_MODEL), F32)],
        compiler_params=pltpu.CompilerParams(
            dimension_semantics=("arbitrary",), vmem_limit_bytes=VMEM_LIMIT),
        name="sample_ffn",
    )(x1, g_ffn, w_up, w_up, conv_w, conv_w, conv_b, conv_b,
      state2d, state2d, state2d, state2d, w_down, g_final)


def kernel(x_prompt, x_sample, cache_swa_k, cache_swa_v, state_ffn_conv, g_mix, w_in, ln_v_g, ln_v_b,
           w_s, b_s, sinks, w_pa, w_pb, w_o, g_ffn, w_up, conv_w, conv_b, w_down, g_final):
    depth = g_mix.shape[0]
    assert depth == 1
    nb = x_prompt.shape[0]
    nd = x_sample.shape[0]
    half = HEAD_DIM // 2
    inv = ROPE_THETA ** (-jnp.arange(half, dtype=F32) / half)
    inv_row = jnp.tile(inv, LANES // half)[None, :]

    l = 0
    row = lambda v: v[l][None, :]
    w_in_b = w_in[l].astype(BF16)
    w_pa_b = w_pa[l].astype(BF16)
    w_pb_b = w_pb[l].astype(BF16)
    w_o_b = w_o[l].astype(BF16)
    w_up_b = w_up[l].astype(BF16)
    w_dn_b = w_down[l].astype(BF16)
    bs_full = jnp.repeat(b_s[l].T, HEAD_DIM, axis=1)
    ws0 = jnp.repeat(w_s[l][:, 0, 0], HEAD_DIM)[None, :]
    bs0 = bs_full[0:1]

    x1p, kp, vp, gmp = _prompt_mixer(x_prompt, sinks[l], inv_row, row(g_mix), w_in_b, row(ln_v_g), row(ln_v_b),
                                     w_s[l], bs_full, w_pa_b, w_pb_b, w_o_b)
    yp, convp = _prompt_ffn(x1p, row(g_ffn), w_up_b, conv_w[l], row(conv_b), w_dn_b, g_final[None, :])

    xs = x_sample.reshape(nd, D_MODEL)
    ck = cache_swa_k[l].reshape(nd, CHUNK, KV_WIDTH)
    cv = cache_swa_v[l].reshape(nd, CHUNK, KV_WIDTH)
    x1s, gms, ckn, cvn = _sample_mixer(xs, ck, cv, sinks[l], inv_row, row(g_mix), w_in_b, row(ln_v_g),
                                       row(ln_v_b), ws0, bs0, w_pa_b, w_pb_b, w_o_b)
    state = state_ffn_conv[l]
    ys, za, zb = _sample_ffn(x1s, state.reshape(nd, (CONV_W - 1) * 2 * D_FF), row(g_ffn), w_up_b,
                             conv_w[l], row(conv_b), w_dn_b, g_final[None, :])
    conv_s = jnp.stack([state[:, 1, :], jnp.concatenate([za, zb], axis=1)], axis=1)

    return (yp,
            ys.reshape(nd, 1, D_MODEL),
            kp.reshape(1, nb, CHUNK, KV_HEADS, HEAD_DIM),
            vp.reshape(1, nb, CHUNK, KV_HEADS, HEAD_DIM),
            gmp.reshape(1, nb, CHUNK, GM_HEADS, GM_WIDTH // GM_HEADS),
            convp[None],
            ckn.reshape(1, nd, CHUNK, KV_HEADS, HEAD_DIM),
            cvn.reshape(1, nd, CHUNK, KV_HEADS, HEAD_DIM),
            gms.reshape(1, nd, 1, GM_HEADS, GM_WIDTH // GM_HEADS),
            conv_s[None])
```

```python
import functools
import math

import jax
import jax.numpy as jnp
from jax import lax
from jax.experimental import pallas as pl
from jax.experimental.pallas import tpu as pltpu

F32 = jnp.float32
BF16 = jnp.bfloat16

D_MODEL = 1024
SEQ = 2048
PAST_LEN = 16384
CHUNK = 128
GM_HEADS = 8
GM_WIDTH = 512
N_HEADS = 8
KV_HEADS = 2
HEAD_DIM = 64
Q_WIDTH = N_HEADS * HEAD_DIM
KV_WIDTH = KV_HEADS * HEAD_DIM
D_FF = 3 * D_MODEL
CONV_W = 3
EPS = 1e-6
ROPE_THETA = 10000.0
LANES = 128
SUBLANES = 8
OFF_Q = 2 * GM_WIDTH
OFF_K = OFF_Q + Q_WIDTH
OFF_V = OFF_K + KV_WIDTH
OFF_GA = OFF_V + KV_WIDTH
OFF_GB = OFF_GA + D_MODEL
IN_WIDTH = OFF_GB + D_MODEL

PROMPT_TILE = 512
SAMPLE_CHUNK = 16
SAMPLE_UNROLL = 8
FF_CHUNK = 512
FFN_LAG = 4
ROW_PHASES = 2
VMEM_LIMIT = 52 * 1024 * 1024

_GELU_C0 = math.sqrt(2.0 / math.pi)
_GELU_C1 = _GELU_C0 * 0.044715


def _gelu(x):
    inner = x * (_GELU_C0 + _GELU_C1 * (x * x))
    hx = 0.5 * x
    return hx + hx * jnp.tanh(inner)


def _rms(x, g):
    ms = jnp.mean(x * x, axis=-1, keepdims=True)
    return x * lax.rsqrt(ms + EPS) * g


def _layer_norm(x, g, b):
    mu = jnp.mean(x, axis=-1, keepdims=True)
    xc = x - mu
    var = jnp.mean(xc * xc, axis=-1, keepdims=True)
    return xc * lax.rsqrt(var + EPS) * g + b


def _sigmoid(x):
    return 1.0 / (1.0 + jnp.exp(-x))


def _dot(a, b):
    return jnp.dot(a, b, preferred_element_type=F32)


def _dot_nt(a, b):
    return lax.dot_general(a, b, (((1,), (1,)), ((), ())), preferred_element_type=F32)


def _rope(xc, cos, sin_signed, first_half):
    partner = jnp.where(first_half, pltpu.roll(xc, LANES - 32, 1), pltpu.roll(xc, 32, 1))
    return xc * cos + partner * sin_signed


def _rope_tables(pos_f32, inv_row, shape):
    lane = lax.broadcasted_iota(jnp.int32, shape, 1)
    ang = pos_f32 * inv_row
    sin = jnp.sin(ang)
    return jnp.cos(ang), jnp.where((lane % HEAD_DIM) < HEAD_DIM // 2, -sin, sin)


def _prompt_mixer_kernel(sinks_ref, x_ref, inv_ref, g_ref, win_ref, lng_ref, lnb_ref, ws_ref, bs_ref,
                         wpa_ref, wpb_ref, wo_ref,
                         x1_ref, klast_ref, vlast_ref, gmlast_ref,
                         cos_ref, sin_ref, wcat_ref, kprev_ref, vprev_ref, h_ref, *, tile):
    b = pl.program_id(0)
    s = pl.program_id(1)
    nblk = tile // CHUNK

    @pl.when((b == 0) & (s == 0))
    def _init_tables():
        def body(i, carry):
            r0 = pl.multiple_of(i * CHUNK, CHUNK)
            pos = (lax.broadcasted_iota(jnp.int32, (CHUNK, LANES), 0) + r0).astype(F32)
            cos, sin_signed = _rope_tables(pos, inv_ref[...], (CHUNK, LANES))
            cos_ref[pl.ds(r0, CHUNK), :] = cos
            sin_ref[pl.ds(r0, CHUNK), :] = sin_signed
            return carry
        lax.fori_loop(0, SEQ // CHUNK, body, 0)
        row = lax.broadcasted_iota(jnp.int32, (CHUNK, CHUNK), 0)
        col = lax.broadcasted_iota(jnp.int32, (CHUNK, CHUNK), 1)
        causal = col <= row
        for j in range(GM_HEADS // 2):
            wcat_ref[j, :, 0:CHUNK] = jnp.where(causal, ws_ref[2 * j], 0.0).astype(BF16)
            wcat_ref[j, :, CHUNK:2 * CHUNK] = jnp.where(causal, ws_ref[2 * j + 1], 0.0).astype(BF16)

    @pl.when(s == 0)
    def _reset_carry():
        kprev_ref[...] = jnp.zeros_like(kprev_ref)
        vprev_ref[...] = jnp.zeros_like(vprev_ref)

    lane = lax.broadcasted_iota(jnp.int32, (CHUNK, LANES), 1)
    low = lane < HEAD_DIM
    low2 = lax.broadcasted_iota(jnp.int32, (2 * CHUNK, LANES), 1) < HEAD_DIM
    first_half_t = (lax.broadcasted_iota(jnp.int32, (tile, LANES), 1) % HEAD_DIM) < HEAD_DIM // 2
    row = lax.broadcasted_iota(jnp.int32, (CHUNK, CHUNK), 0)
    col = lax.broadcasted_iota(jnp.int32, (CHUNK, CHUNK), 1)
    cur_mask = col <= row
    prev_live = (col - row) >= jnp.where(s > 0, 0, 2 * CHUNK)
    prev_inner = col >= row

    x = x_ref[...]
    h_ref[...] = _rms(x, g_ref[...]).astype(BF16)

    zqkv = _dot(h_ref[...], win_ref[:, OFF_Q:OFF_GA])
    zuv = _dot(h_ref[...], win_ref[:, 0:OFF_Q])

    r0 = pl.multiple_of(s * tile, tile)
    cos = cos_ref[pl.ds(r0, tile), :]
    sin_signed = sin_ref[pl.ds(r0, tile), :]
    qcols = [(_rope(zqkv[:, j * LANES:(j + 1) * LANES], cos, sin_signed, first_half_t)
              * (HEAD_DIM ** -0.5)).astype(BF16) for j in range(Q_WIDTH // LANES)]
    krot = _rope(zqkv[:, Q_WIDTH:Q_WIDTH + KV_WIDTH], cos, sin_signed, first_half_t)
    vval = zqkv[:, Q_WIDTH + KV_WIDTH:Q_WIDTH + 2 * KV_WIDTH]

    kp = kprev_ref[...]
    vp = vprev_ref[...]
    scores = {}
    vblks = {}
    for i in range(nblk):
        r = slice(i * CHUNK, (i + 1) * CHUNK)
        kk = jnp.concatenate([kp, krot[r]], axis=0)
        vv = jnp.concatenate([vp, vval[r]], axis=0)
        kk_sw = pltpu.roll(kk, HEAD_DIM, 1)
        vv_sw = pltpu.roll(vv, HEAD_DIM, 1)
        for g in range(KV_HEADS):
            k_lo = jnp.where(low2, kk if g == 0 else kk_sw, 0.0).astype(BF16)
            k_hi = jnp.where(low2, 0.0, kk_sw if g == 0 else kk).astype(BF16)
            v_lo = jnp.where(low2, vv if g == 0 else vv_sw, 0.0).astype(BF16)
            v_hi = jnp.where(low2, 0.0, vv_sw if g == 0 else vv).astype(BF16)
            krhs = jnp.concatenate([k_lo, k_hi], axis=0)
            vblks[i, g] = jnp.concatenate([v_lo, v_hi], axis=0)
            qstack = jnp.concatenate([qcols[2 * g][r], qcols[2 * g + 1][r]], axis=0)
            scores[i, g] = _dot_nt(qstack, krhs)
        kp = krot[r]
        vp = vval[r]
    kprev_ref[...] = kp
    vprev_ref[...] = vp

    gates = _dot(h_ref[...], win_ref[:, OFF_GA:IN_WIDTH])

    a = _gelu(zuv)
    u = a[:, 0:GM_WIDTH]
    vg = _layer_norm(a[:, GM_WIDTH:OFF_Q], lng_ref[...], lnb_ref[...])
    ya_blocks = []
    for c in range(nblk):
        r = slice(c * CHUNK, (c + 1) * CHUNK)
        cols = []
        for j in range(GM_HEADS // 2):
            vcol = vg[r, j * LANES:(j + 1) * LANES]
            rhs = jnp.concatenate([jnp.where(low, vcol, 0.0), jnp.where(low, 0.0, vcol)], axis=0).astype(BF16)
            cols.append(_dot(wcat_ref[j], rhs))
        mix = jnp.concatenate(cols, axis=1) + bs_ref[...]
        ya_blocks.append((u[r] * mix).astype(BF16))
    ya = jnp.concatenate(ya_blocks, axis=0)
    pa = _dot(ya, wpa_ref[...])

    yb_blocks = []
    for i in range(nblk):
        pmask = prev_live if i == 0 else prev_inner
        cols_out = []
        for g in range(KV_HEADS):
            sc = scores[i, g]
            p_rows = []
            rinv_rows = []
            for jj in range(2):
                ps = []
                ls = []
                for par in range(2):
                    hd = 4 * g + 2 * jj + par
                    sh = sc[jj * CHUNK:(jj + 1) * CHUNK, par * 2 * CHUNK:(par + 1) * 2 * CHUNK]
                    sp = jnp.where(pmask, sh[:, 0:CHUNK], -jnp.inf)
                    scur = jnp.where(cur_mask, sh[:, CHUNK:2 * CHUNK], -jnp.inf)
                    sink = sinks_ref[hd]
                    m = jnp.maximum(jnp.max(jnp.maximum(sp, scur), axis=-1, keepdims=True), sink)
                    pp = jnp.exp(sp - m)
                    pc = jnp.exp(scur - m)
                    lsum = jnp.sum(pp + pc, axis=-1, keepdims=True) + jnp.exp(sink - m)
                    ps.append(pp.astype(BF16))
                    ps.append(pc.astype(BF16))
                    ls.append(lsum)
                p_rows.append(jnp.concatenate(ps, axis=1))
                rinv_rows.append(jnp.where(low, 1.0 / ls[0], 1.0 / ls[1]))
            o = _dot(jnp.concatenate(p_rows, axis=0), vblks[i, g])
            cols_out.append(o[0:CHUNK] * rinv_rows[0])
            cols_out.append(o[CHUNK:2 * CHUNK] * rinv_rows[1])
        yb_blocks.append(jnp.concatenate(cols_out, axis=1).astype(BF16))
    yb = jnp.concatenate(yb_blocks, axis=0)

    merged = (_sigmoid(gates[:, 0:D_MODEL]) * pa
              + _sigmoid(gates[:, D_MODEL:2 * D_MODEL]) * _dot(yb, wpb_ref[...]))
    x1_ref[...] = x + _dot(merged.astype(BF16), wo_ref[...])

    @pl.when(s == pl.num_programs(1) - 1)
    def _emit_state():
        klast_ref[...] = krot[tile - CHUNK:tile]
        vlast_ref[...] = vval[tile - CHUNK:tile]
        gmlast_ref[...] = vg[tile - CHUNK:tile]


def _const_spec(shape):
    nd = len(shape)
    return pl.BlockSpec(shape, lambda *_: (0,) * nd, pipeline_mode=pl.Buffered(1))


def _prompt_mixer(x, sinks, inv_row, g_mix, w_in, ln_g, ln_b, w_s, bs_full, w_pa, w_pb, w_o):
    nb, seq, _ = x.shape
    tile = PROMPT_TILE
    grid = (nb, seq // tile)
    row_spec = pl.BlockSpec((None, tile, D_MODEL), lambda b, s: (b, s, 0))
    last = lambda w: pl.BlockSpec((None, CHUNK, w), lambda b, s: (b, 0, 0))
    return pl.pallas_call(
        functools.partial(_prompt_mixer_kernel, tile=tile),
        grid=grid,
        in_specs=[
            pl.BlockSpec(memory_space=pltpu.SMEM),
            row_spec,
            _const_spec((1, LANES)),
            _const_spec((1, D_MODEL)),
            _const_spec((D_MODEL, IN_WIDTH)),
            _const_spec((1, GM_WIDTH)),
            _const_spec((1, GM_WIDTH)),
            _const_spec((GM_HEADS, CHUNK, CHUNK)),
            _const_spec((CHUNK, GM_WIDTH)),
            _const_spec((GM_WIDTH, D_MODEL)),
            _const_spec((Q_WIDTH, D_MODEL)),
            _const_spec((D_MODEL, D_MODEL)),
        ],
        out_specs=[row_spec, last(KV_WIDTH), last(KV_WIDTH), last(GM_WIDTH)],
        out_shape=[
            jax.ShapeDtypeStruct((nb, seq, D_MODEL), F32),
            jax.ShapeDtypeStruct((nb, CHUNK, KV_WIDTH), F32),
            jax.ShapeDtypeStruct((nb, CHUNK, KV_WIDTH), F32),
            jax.ShapeDtypeStruct((nb, CHUNK, GM_WIDTH), F32),
        ],
        scratch_shapes=[
            pltpu.VMEM((SEQ, LANES), F32),
            pltpu.VMEM((SEQ, LANES), F32),
            pltpu.VMEM((GM_HEADS // 2, CHUNK, 2 * CHUNK), BF16),
            pltpu.VMEM((CHUNK, KV_WIDTH), F32),
            pltpu.VMEM((CHUNK, KV_WIDTH), F32),
            pltpu.VMEM((tile, D_MODEL), BF16),
        ],
        compiler_params=pltpu.CompilerParams(
            dimension_semantics=("arbitrary", "arbitrary"), vmem_limit_bytes=VMEM_LIMIT),
        name="prompt_mixer",
    )(sinks, x, inv_row, g_mix, w_in, ln_g, ln_b, w_s, bs_full, w_pa, w_pb, w_o)


def _prompt_ffn_kernel(x_ref, g_ref, wup_ref, cw_ref, cb_ref, wdn_ref, gf_ref,
                       y_ref, convlast_ref, zbuf_ref, gbuf_ref, ybuf_ref, h_ref, *, tile):
    s = pl.program_id(1)
    pad = SUBLANES
    q = tile // ROW_PHASES
    nslab_half = D_FF // LANES

    @pl.when(s == 0)
    def _reset_carry():
        zbuf_ref[:, 0:pad, :] = jnp.zeros((2 * nslab_half, pad, LANES), F32)

    x = x_ref[...]
    h_ref[...] = _rms(x, g_ref[...]).astype(BF16)

    def up_project(j):
        for half in range(2):
            n0 = half * D_FF + j * FF_CHUNK
            z = _dot(h_ref[...], wup_ref[:, n0:n0 + FF_CHUNK])
            for jj in range(FF_CHUNK // LANES):
                zbuf_ref[n0 // LANES + jj, pad:pad + tile, :] = z[:, jj * LANES:(jj + 1) * LANES]

    def conv(slab):
        cs = slice(slab * LANES, (slab + 1) * LANES)
        rows = [zbuf_ref[slab, pl.ds(pad - (CONV_W - 1) + m, q, stride=ROW_PHASES), :]
                for m in range(ROW_PHASES + CONV_W - 1)]
        taps = [cw_ref[t:t + 1, cs] for t in range(CONV_W)]
        bias = cb_ref[:, cs]
        return [bias + taps[0] * rows[r] + taps[1] * rows[r + 1] + taps[2] * rows[r + 2]
                for r in range(ROW_PHASES)]

    def gate_and_project(j, yperm):
        for ja in range(j * (FF_CHUNK // LANES), (j + 1) * (FF_CHUNK // LANES)):
            ca = conv(ja)
            cb = conv(ja + nslab_half)
            for r in range(ROW_PHASES):
                gbuf_ref[r * q:(r + 1) * q, ja * LANES:(ja + 1) * LANES] = (_gelu(ca[r]) * cb[r]).astype(BF16)
        cs = slice(j * FF_CHUNK, (j + 1) * FF_CHUNK)
        part = _dot(gbuf_ref[:, cs], wdn_ref[cs, :])
        return part if yperm is None else yperm + part

    nchunk = D_FF // FF_CHUNK
    yperm = None
    for j in range(nchunk + FFN_LAG):
        if j < nchunk:
            up_project(j)
        if j >= FFN_LAG:
            yperm = gate_and_project(j - FFN_LAG, yperm)

    for j in range(D_MODEL // LANES):
        for r in range(ROW_PHASES):
            ybuf_ref[j, pl.ds(r, q, stride=ROW_PHASES), :] = yperm[r * q:(r + 1) * q, j * LANES:(j + 1) * LANES]
    y = jnp.concatenate([ybuf_ref[j] for j in range(D_MODEL // LANES)], axis=1)
    y_ref[...] = _rms(x + y, gf_ref[...])

    @pl.when(s == pl.num_programs(1) - 1)
    def _emit_state():
        for slab in range(2 * nslab_half):
            convlast_ref[:, slab * LANES:(slab + 1) * LANES] = (
                zbuf_ref[slab, pad + tile - (CONV_W - 1):pad + tile, :])

    zbuf_ref[:, 0:pad, :] = zbuf_ref[:, tile:tile + pad, :]


def _prompt_ffn(x1, g_ffn, w_up, conv_w, conv_b, w_down, g_final):
    nb, seq, _ = x1.shape
    tile = PROMPT_TILE
    row_spec = pl.BlockSpec((None, tile, D_MODEL), lambda b, s: (b, s, 0))
    return pl.pallas_call(
        functools.partial(_prompt_ffn_kernel, tile=tile),
        grid=(nb, seq // tile),
        in_specs=[
            row_spec,
            _const_spec((1, D_MODEL)),
            _const_spec((D_MODEL, 2 * D_FF)),
            _const_spec((CONV_W, 2 * D_FF)),
            _const_spec((1, 2 * D_FF)),
            _const_spec((D_FF, D_MODEL)),
            _const_spec((1, D_MODEL)),
        ],
        out_specs=[row_spec, pl.BlockSpec((None, CONV_W - 1, 2 * D_FF), lambda b, s: (b, 0, 0))],
        out_shape=[
            jax.ShapeDtypeStruct((nb, seq, D_MODEL), F32),
            jax.ShapeDtypeStruct((nb, CONV_W - 1, 2 * D_FF), F32),
        ],
        scratch_shapes=[
            pltpu.VMEM((2 * D_FF // LANES, tile + SUBLANES, LANES), F32),
            pltpu.VMEM((tile, D_FF), BF16),
            pltpu.VMEM((D_MODEL // LANES, tile, LANES), F32),
            pltpu.VMEM((tile, D_MODEL), BF16),
        ],
        compiler_params=pltpu.CompilerParams(
            dimension_semantics=("arbitrary", "arbitrary"), vmem_limit_bytes=VMEM_LIMIT),
        name="prompt_ffn",
    )(x1, g_ffn, w_up, conv_w, conv_b, w_down, g_final)


def _sample_mixer_kernel(sinks_ref, x_ref, ck_ref, cv_ref, inv_ref, g_ref, win_ref, lng_ref, lnb_ref,
                         ws0_ref, bs0_ref, wpa_ref, wpb_ref, wo_ref,
                         x1_ref, gm_ref, ckn_ref, cvn_ref,
                         qz_ref, kn_ref, vn_ref, o_ref, *, chunk):
    lane = lax.broadcasted_iota(jnp.int32, (chunk, LANES), 1)
    low = lane < HEAD_DIM
    first_half = (lane % HEAD_DIM) < HEAD_DIM // 2

    x = x_ref[...]
    h = _rms(x, g_ref[...]).astype(BF16)
    a = _gelu(_dot(h, win_ref[:, 0:OFF_Q]))
    u = a[:, 0:GM_WIDTH]
    vg = _layer_norm(a[:, GM_WIDTH:OFF_Q], lng_ref[...], lnb_ref[...])
    gm_ref[...] = vg
    ya = (u * (vg * ws0_ref[...] + bs0_ref[...])).astype(BF16)

    zqkv = _dot(h, win_ref[:, OFF_Q:OFF_GA])
    pos = jnp.full((chunk, LANES), float(PAST_LEN), F32)
    cos, sin_signed = _rope_tables(pos, inv_ref[...], (chunk, LANES))
    knew = _rope(zqkv[:, Q_WIDTH:Q_WIDTH + KV_WIDTH], cos, sin_signed, first_half)
    vnew = zqkv[:, Q_WIDTH + KV_WIDTH:Q_WIDTH + 2 * KV_WIDTH]
    kn_ref[...] = knew
    vn_ref[...] = vnew
    for j in range(Q_WIDTH // LANES):
        qc = _rope(zqkv[:, j * LANES:(j + 1) * LANES], cos, sin_signed, first_half) * (HEAD_DIM ** -0.5)
        qsw = pltpu.roll(qc, HEAD_DIM, 1)
        g = j // 2
        even = jnp.where(low, qc, 0.0) if g == 0 else jnp.where(low, 0.0, qsw)
        odd = jnp.where(low, qsw, 0.0) if g == 0 else jnp.where(low, 0.0, qc)
        qz_ref[(2 * j) * chunk:(2 * j + 1) * chunk, :] = even
        qz_ref[(2 * j + 1) * chunk:(2 * j + 2) * chunk, :] = odd

    hrow = lax.broadcasted_iota(jnp.int32, (N_HEADS, 1), 0)
    sink_col = jnp.zeros((N_HEADS, 1), F32)
    for hd in range(N_HEADS):
        sink_col = jnp.where(hrow == hd, sinks_ref[hd], sink_col)
    head_low = lax.broadcasted_iota(jnp.int32, (N_HEADS, LANES), 0) < N_HEADS // KV_HEADS
    lane_low = lax.broadcasted_iota(jnp.int32, (N_HEADS, LANES), 1) < HEAD_DIM
    own_half = head_low == lane_low
    krow = lax.broadcasted_iota(jnp.int32, (CHUNK, LANES), 0)

    def body(bi, carry):
        qz = qz_ref[pl.ds(bi, N_HEADS, stride=chunk), :]
        kb = ck_ref[bi]
        vb = cv_ref[bi]
        kn = kn_ref[pl.ds(bi, 1), :]
        vn = vn_ref[pl.ds(bi, 1), :]
        sc = _dot_nt(qz.astype(BF16), kb.astype(BF16))
        sn = jnp.sum(qz * kn, axis=-1, keepdims=True)
        m = jnp.maximum(jnp.maximum(jnp.max(sc, axis=-1, keepdims=True), sn), sink_col)
        p = jnp.exp(sc - m)
        pn = jnp.exp(sn - m)
        lsum = jnp.sum(p, axis=-1, keepdims=True) + pn + jnp.exp(sink_col - m)
        o = (_dot(p.astype(BF16), vb.astype(BF16)) + pn * vn) / lsum
        o_ref[pl.ds(bi, N_HEADS, stride=chunk), :] = jnp.where(own_half, o, 0.0)
        ckn_ref[bi] = jnp.where(krow == CHUNK - 1, kn, pltpu.roll(kb, CHUNK - 1, 0))
        cvn_ref[bi] = jnp.where(krow == CHUNK - 1, vn, pltpu.roll(vb, CHUNK - 1, 0))
        return carry

    lax.fori_loop(0, chunk, body, 0, unroll=SAMPLE_UNROLL)

    cols = []
    for j in range(Q_WIDTH // LANES):
        g = j // 2
        oe = o_ref[(2 * j) * chunk:(2 * j + 1) * chunk, :]
        oo = o_ref[(2 * j + 1) * chunk:(2 * j + 2) * chunk, :]
        if g == 0:
            cols.append(jnp.where(low, oe, pltpu.roll(oo, HEAD_DIM, 1)))
        else:
            cols.append(jnp.where(low, pltpu.roll(oe, HEAD_DIM, 1), oo))
    yb = jnp.concatenate(cols, axis=1).astype(BF16)

    gates = _dot(h, win_ref[:, OFF_GA:IN_WIDTH])
    merged = (_sigmoid(gates[:, 0:D_MODEL]) * _dot(ya, wpa_ref[...])
              + _sigmoid(gates[:, D_MODEL:2 * D_MODEL]) * _dot(yb, wpb_ref[...]))
    x1_ref[...] = x + _dot(merged.astype(BF16), wo_ref[...])


def _sample_mixer(x, ck, cv, sinks, inv_row, g_mix, w_in, ln_g, ln_b, ws0, bs0, w_pa, w_pb, w_o):
    n = x.shape[0]
    chunk = SAMPLE_CHUNK
    row = lambda w: pl.BlockSpec((chunk, w), lambda i: (i, 0))
    cache = pl.BlockSpec((chunk, CHUNK, KV_WIDTH), lambda i: (i, 0, 0))
    return pl.pallas_call(
        functools.partial(_sample_mixer_kernel, chunk=chunk),
        grid=(n // chunk,),
        in_specs=[
            pl.BlockSpec(memory_space=pltpu.SMEM),
            row(D_MODEL), cache, cache,
            _const_spec((1, LANES)),
            _const_spec((1, D_MODEL)),
            _const_spec((D_MODEL, IN_WIDTH)),
            _const_spec((1, GM_WIDTH)),
            _const_spec((1, GM_WIDTH)),
            _const_spec((1, GM_WIDTH)),
            _const_spec((1, GM_WIDTH)),
            _const_spec((GM_WIDTH, D_MODEL)),
            _const_spec((Q_WIDTH, D_MODEL)),
            _const_spec((D_MODEL, D_MODEL)),
        ],
        out_specs=[row(D_MODEL), row(GM_WIDTH), cache, cache],
        out_shape=[
            jax.ShapeDtypeStruct((n, D_MODEL), F32),
            jax.ShapeDtypeStruct((n, GM_WIDTH), F32),
            jax.ShapeDtypeStruct((n, CHUNK, KV_WIDTH), F32),
            jax.ShapeDtypeStruct((n, CHUNK, KV_WIDTH), F32),
        ],
        scratch_shapes=[
            pltpu.VMEM((N_HEADS * chunk, LANES), F32),
            pltpu.VMEM((chunk, KV_WIDTH), F32),
            pltpu.VMEM((chunk, KV_WIDTH), F32),
            pltpu.VMEM((N_HEADS * chunk, LANES), F32),
        ],
        compiler_params=pltpu.CompilerParams(
            dimension_semantics=("arbitrary",), vmem_limit_bytes=VMEM_LIMIT),
        name="sample_mixer",
    )(sinks, x, ck, cv, inv_row, g_mix, w_in, ln_g, ln_b, ws0, bs0, w_pa, w_pb, w_o)


def _sample_ffn_kernel(x_ref, g_ref, wua_ref, wub_ref, cwa_ref, cwb_ref, cba_ref, cbb_ref,
                       p0a_ref, p0b_ref, p1a_ref, p1b_ref, wdn_ref, gf_ref,
                       y_ref, za_ref, zb_ref, acc_ref):
    j = pl.program_id(0)
    x = x_ref[...]
    h = _rms(x, g_ref[...]).astype(BF16)
    za = _dot(h, wua_ref[...])
    zb = _dot(h, wub_ref[...])
    za_ref[...] = za
    zb_ref[...] = zb
    ca = cba_ref[...] + cwa_ref[0:1, :] * p0a_ref[...] + cwa_ref[1:2, :] * p1a_ref[...] + cwa_ref[2:3, :] * za
    cb = cbb_ref[...] + cwb_ref[0:1, :] * p0b_ref[...] + cwb_ref[1:2, :] * p1b_ref[...] + cwb_ref[2:3, :] * zb
    part = _dot((_gelu(ca) * cb).astype(BF16), wdn_ref[...])

    @pl.when(j == 0)
    def _first():
        acc_ref[...] = part

    @pl.when(j > 0)
    def _rest():
        acc_ref[...] += part

    @pl.when(j == pl.num_programs(0) - 1)
    def _finish():
        y_ref[...] = _rms(x + acc_ref[...], gf_ref[...])


def _sample_ffn(x1, state2d, g_ffn, w_up, conv_w, conv_b, w_down, g_final):
    n = x1.shape[0]
    nchunk = D_FF // FF_CHUNK
    full = lambda w: pl.BlockSpec((n, w), lambda j: (0, 0))
    col = lambda rows, off: pl.BlockSpec((rows, FF_CHUNK), lambda j, off=off: (0, j + off))
    return pl.pallas_call(
        _sample_ffn_kernel,
        grid=(nchunk,),
        in_specs=[
            full(D_MODEL),
            pl.BlockSpec((1, D_MODEL), lambda j: (0, 0)),
            col(D_MODEL, 0), col(D_MODEL, nchunk),
            col(CONV_W, 0), col(CONV_W, nchunk),
            col(1, 0), col(1, nchunk),
            col(n, 0), col(n, nchunk), col(n, 2 * nchunk), col(n, 3 * nchunk),
            pl.BlockSpec((FF_CHUNK, D_MODEL), lambda j: (j, 0)),
            pl.BlockSpec((1, D_MODEL), lambda j: (0, 0)),
        ],
        out_specs=[full(D_MODEL), col(n, 0), col(n, 0)],
        out_shape=[
            jax.ShapeDtypeStruct((n, D_MODEL), F32),
            jax.ShapeDtypeStruct((n, D_FF), F32),
            jax.ShapeDtypeStruct((n, D_FF), F32),
        ],
        scratch_shapes=[pltpu.VMEM((n, D_MODEL), F32)],
        compiler_params=pltpu.CompilerParams(
            dimension_semantics=("arbitrary",), vmem_limit_bytes=VMEM_LIMIT),
        name="sample_ffn",
    )(x1, g_ffn, w_up, w_up, conv_w, conv_w, conv_b, conv_b,
      state2d, state2d, state2d, state2d, w_down, g_final)


def kernel(x_prompt, x_sample, cache_swa_k, cache_swa_v, state_ffn_conv, g_mix, w_in, ln_v_g, ln_v_b,
           w_s, b_s, sinks, w_pa, w_pb, w_o, g_ffn, w_up, conv_w, conv_b, w_down, g_final):
    depth = g_mix.shape[0]
    assert depth == 1
    nb = x_prompt.shape[0]
    nd = x_sample.shape[0]
    half = HEAD_DIM // 2
    inv = ROPE_THETA ** (-jnp.arange(half, dtype=F32) / half)
    inv_row = jnp.tile(inv, LANES // half)[None, :]

    l = 0
    row = lambda v: v[l][None, :]
    w_in_b = w_in[l].astype(BF16)
    w_pa_b = w_pa[l].astype(BF16)
    w_pb_b = w_pb[l].astype(BF16)
    w_o_b = w_o[l].astype(BF16)
    w_up_b = w_up[l].astype(BF16)
    w_dn_b = w_down[l].astype(BF16)
    bs_full = jnp.repeat(b_s[l].T, HEAD_DIM, axis=1)
    ws0 = jnp.repeat(w_s[l][:, 0, 0], HEAD_DIM)[None, :]
    bs0 = bs_full[0:1]

    x1p, kp, vp, gmp = _prompt_mixer(x_prompt, sinks[l], inv_row, row(g_mix), w_in_b, row(ln_v_g), row(ln_v_b),
                                     w_s[l], bs_full, w_pa_b, w_pb_b, w_o_b)
    yp, convp = _prompt_ffn(x1p, row(g_ffn), w_up_b, conv_w[l], row(conv_b), w_dn_b, g_final[None, :])

    xs = x_sample.reshape(nd, D_MODEL)
    ck = cache_swa_k[l].reshape(nd, CHUNK, KV_WIDTH)
    cv = cache_swa_v[l].reshape(nd, CHUNK, KV_WIDTH)
    x1s, gms, ckn, cvn = _sample_mixer(xs, ck, cv, sinks[l], inv_row, row(g_mix), w_in_b, row(ln_v_g),
                                       row(ln_v_b), ws0, bs0, w_pa_b, w_pb_b, w_o_b)
    state = state_ffn_conv[l]
    ys, za, zb = _sample_ffn(x1s, state.reshape(nd, (CONV_W - 1) * 2 * D_FF), row(g_ffn), w_up_b,
                             conv_w[l], row(conv_b), w_dn_b, g_final[None, :])
    conv_s = jnp.stack([state[:, 1, :], jnp.concatenate([za, zb], axis=1)], axis=1)

    return (yp,
            ys.reshape(nd, 1, D_MODEL),
            kp.reshape(1, nb, CHUNK, KV_HEADS, HEAD_DIM),
            vp.reshape(1, nb, CHUNK, KV_HEADS, HEAD_DIM),
            gmp.reshape(1, nb, CHUNK, GM_HEADS, GM_WIDTH // GM_HEADS),
            convp[None],
            ckn.reshape(1, nd, CHUNK, KV_HEADS, HEAD_DIM),
            cvn.reshape(1, nd, CHUNK, KV_HEADS, HEAD_DIM),
            gms.reshape(1, nd, 1, GM_HEADS, GM_WIDTH // GM_HEADS),
            conv_s[None])
```

```python
import functools
import math

import jax
import jax.numpy as jnp
from jax import lax
from jax.experimental import pallas as pl
from jax.experimental.pallas import tpu as pltpu

F32 = jnp.float32
BF16 = jnp.bfloat16

D_MODEL = 1024
SEQ = 2048
PAST_LEN = 16384
CHUNK = 128
GM_HEADS = 8
GM_WIDTH = 512
N_HEADS = 8
KV_HEADS = 2
HEAD_DIM = 64
Q_WIDTH = N_HEADS * HEAD_DIM
KV_WIDTH = KV_HEADS * HEAD_DIM
D_FF = 3 * D_MODEL
CONV_W = 3
EPS = 1e-6
ROPE_THETA = 10000.0
LANES = 128
SUBLANES = 8
OFF_Q = 2 * GM_WIDTH
OFF_K = OFF_Q + Q_WIDTH
OFF_V = OFF_K + KV_WIDTH
OFF_GA = OFF_V + KV_WIDTH
OFF_GB = OFF_GA + D_MODEL
IN_WIDTH = OFF_GB + D_MODEL

PROMPT_TILE = 512
SAMPLE_CHUNK = 16
SAMPLE_UNROLL = 8
FF_CHUNK = 512
FFN_LAG = 4
ROW_PHASES = 4
VMEM_LIMIT = 52 * 1024 * 1024

_GELU_C0 = math.sqrt(2.0 / math.pi)
_GELU_C1 = _GELU_C0 * 0.044715


def _gelu(x):
    inner = x * (_GELU_C0 + _GELU_C1 * (x * x))
    hx = 0.5 * x
    return hx + hx * jnp.tanh(inner)


def _rms(x, g):
    ms = jnp.mean(x * x, axis=-1, keepdims=True)
    return x * lax.rsqrt(ms + EPS) * g


def _layer_norm(x, g, b):
    mu = jnp.mean(x, axis=-1, keepdims=True)
    xc = x - mu
    var = jnp.mean(xc * xc, axis=-1, keepdims=True)
    return xc * lax.rsqrt(var + EPS) * g + b


def _sigmoid(x):
    return 1.0 / (1.0 + jnp.exp(-x))


def _dot(a, b):
    return jnp.dot(a, b, preferred_element_type=F32)


def _dot_nt(a, b):
    return lax.dot_general(a, b, (((1,), (1,)), ((), ())), preferred_element_type=F32)


def _rope(xc, cos, sin_signed, first_half):
    partner = jnp.where(first_half, pltpu.roll(xc, LANES - 32, 1), pltpu.roll(xc, 32, 1))
    return xc * cos + partner * sin_signed


def _rope_tables(pos_f32, inv_row, shape):
    lane = lax.broadcasted_iota(jnp.int32, shape, 1)
    ang = pos_f32 * inv_row
    sin = jnp.sin(ang)
    return jnp.cos(ang), jnp.where((lane % HEAD_DIM) < HEAD_DIM // 2, -sin, sin)


def _prompt_mixer_kernel(sinks_ref, x_ref, inv_ref, g_ref, win_ref, lng_ref, lnb_ref, ws_ref, bs_ref,
                         wpa_ref, wpb_ref, wo_ref,
                         x1_ref, klast_ref, vlast_ref, gmlast_ref,
                         cos_ref, sin_ref, wcat_ref, kprev_ref, vprev_ref, h_ref, *, tile):
    b = pl.program_id(0)
    s = pl.program_id(1)
    nblk = tile // CHUNK

    @pl.when((b == 0) & (s == 0))
    def _init_tables():
        def body(i, carry):
            r0 = pl.multiple_of(i * CHUNK, CHUNK)
            pos = (lax.broadcasted_iota(jnp.int32, (CHUNK, LANES), 0) + r0).astype(F32)
            cos, sin_signed = _rope_tables(pos, inv_ref[...], (CHUNK, LANES))
            cos_ref[pl.ds(r0, CHUNK), :] = cos
            sin_ref[pl.ds(r0, CHUNK), :] = sin_signed
            return carry
        lax.fori_loop(0, SEQ // CHUNK, body, 0)
        row = lax.broadcasted_iota(jnp.int32, (CHUNK, CHUNK), 0)
        col = lax.broadcasted_iota(jnp.int32, (CHUNK, CHUNK), 1)
        causal = col <= row
        for j in range(GM_HEADS // 2):
            wcat_ref[j, :, 0:CHUNK] = jnp.where(causal, ws_ref[2 * j], 0.0).astype(BF16)
            wcat_ref[j, :, CHUNK:2 * CHUNK] = jnp.where(causal, ws_ref[2 * j + 1], 0.0).astype(BF16)

    @pl.when(s == 0)
    def _reset_carry():
        kprev_ref[...] = jnp.zeros_like(kprev_ref)
        vprev_ref[...] = jnp.zeros_like(vprev_ref)

    lane = lax.broadcasted_iota(jnp.int32, (CHUNK, LANES), 1)
    low = lane < HEAD_DIM
    low2 = lax.broadcasted_iota(jnp.int32, (2 * CHUNK, LANES), 1) < HEAD_DIM
    first_half_t = (lax.broadcasted_iota(jnp.int32, (tile, LANES), 1) % HEAD_DIM) < HEAD_DIM // 2
    row = lax.broadcasted_iota(jnp.int32, (CHUNK, CHUNK), 0)
    col = lax.broadcasted_iota(jnp.int32, (CHUNK, CHUNK), 1)
    cur_mask = col <= row
    prev_live = (col - row) >= jnp.where(s > 0, 0, 2 * CHUNK)
    prev_inner = col >= row

    x = x_ref[...]
    h_ref[...] = _rms(x, g_ref[...]).astype(BF16)

    zqkv = _dot(h_ref[...], win_ref[:, OFF_Q:OFF_GA])
    zuv = _dot(h_ref[...], win_ref[:, 0:OFF_Q])

    r0 = pl.multiple_of(s * tile, tile)
    cos = cos_ref[pl.ds(r0, tile), :]
    sin_signed = sin_ref[pl.ds(r0, tile), :]
    qcols = [(_rope(zqkv[:, j * LANES:(j + 1) * LANES], cos, sin_signed, first_half_t)
              * (HEAD_DIM ** -0.5)).astype(BF16) for j in range(Q_WIDTH // LANES)]
    krot = _rope(zqkv[:, Q_WIDTH:Q_WIDTH + KV_WIDTH], cos, sin_signed, first_half_t)
    vval = zqkv[:, Q_WIDTH + KV_WIDTH:Q_WIDTH + 2 * KV_WIDTH]

    kp = kprev_ref[...]
    vp = vprev_ref[...]
    scores = {}
    vblks = {}
    for i in range(nblk):
        r = slice(i * CHUNK, (i + 1) * CHUNK)
        kk = jnp.concatenate([kp, krot[r]], axis=0)
        vv = jnp.concatenate([vp, vval[r]], axis=0)
        kk_sw = pltpu.roll(kk, HEAD_DIM, 1)
        vv_sw = pltpu.roll(vv, HEAD_DIM, 1)
        for g in range(KV_HEADS):
            k_lo = jnp.where(low2, kk if g == 0 else kk_sw, 0.0).astype(BF16)
            k_hi = jnp.where(low2, 0.0, kk_sw if g == 0 else kk).astype(BF16)
            v_lo = jnp.where(low2, vv if g == 0 else vv_sw, 0.0).astype(BF16)
            v_hi = jnp.where(low2, 0.0, vv_sw if g == 0 else vv).astype(BF16)
            krhs = jnp.concatenate([k_lo, k_hi], axis=0)
            vblks[i, g] = jnp.concatenate([v_lo, v_hi], axis=0)
            qstack = jnp.concatenate([qcols[2 * g][r], qcols[2 * g + 1][r]], axis=0)
            scores[i, g] = _dot_nt(qstack, krhs)
        kp = krot[r]
        vp = vval[r]
    kprev_ref[...] = kp
    vprev_ref[...] = vp

    gates = _dot(h_ref[...], win_ref[:, OFF_GA:IN_WIDTH])

    a = _gelu(zuv)
    u = a[:, 0:GM_WIDTH]
    vg = _layer_norm(a[:, GM_WIDTH:OFF_Q], lng_ref[...], lnb_ref[...])
    ya_blocks = []
    for c in range(nblk):
        r = slice(c * CHUNK, (c + 1) * CHUNK)
        cols = []
        for j in range(GM_HEADS // 2):
            vcol = vg[r, j * LANES:(j + 1) * LANES]
            rhs = jnp.concatenate([jnp.where(low, vcol, 0.0), jnp.where(low, 0.0, vcol)], axis=0).astype(BF16)
            cols.append(_dot(wcat_ref[j], rhs))
        mix = jnp.concatenate(cols, axis=1) + bs_ref[...]
        ya_blocks.append((u[r] * mix).astype(BF16))
    ya = jnp.concatenate(ya_blocks, axis=0)
    pa = _dot(ya, wpa_ref[...])

    yb_blocks = []
    for i in range(nblk):
        pmask = prev_live if i == 0 else prev_inner
        cols_out = []
        for g in range(KV_HEADS):
            sc = scores[i, g]
            p_rows = []
            rinv_rows = []
            for jj in range(2):
                ps = []
                ls = []
                for par in range(2):
                    hd = 4 * g + 2 * jj + par
                    sh = sc[jj * CHUNK:(jj + 1) * CHUNK, par * 2 * CHUNK:(par + 1) * 2 * CHUNK]
                    sp = jnp.where(pmask, sh[:, 0:CHUNK], -jnp.inf)
                    scur = jnp.where(cur_mask, sh[:, CHUNK:2 * CHUNK], -jnp.inf)
                    sink = sinks_ref[hd]
                    m = jnp.maximum(jnp.max(jnp.maximum(sp, scur), axis=-1, keepdims=True), sink)
                    pp = jnp.exp(sp - m)
                    pc = jnp.exp(scur - m)
                    lsum = jnp.sum(pp + pc, axis=-1, keepdims=True) + jnp.exp(sink - m)
                    ps.append(pp.astype(BF16))
                    ps.append(pc.astype(BF16))
                    ls.append(lsum)
                p_rows.append(jnp.concatenate(ps, axis=1))
                rinv_rows.append(jnp.where(low, 1.0 / ls[0], 1.0 / ls[1]))
            o = _dot(jnp.concatenate(p_rows, axis=0), vblks[i, g])
            cols_out.append(o[0:CHUNK] * rinv_rows[0])
            cols_out.append(o[CHUNK:2 * CHUNK] * rinv_rows[1])
        yb_blocks.append(jnp.concatenate(cols_out, axis=1).astype(BF16))
    yb = jnp.concatenate(yb_blocks, axis=0)

    merged = (_sigmoid(gates[:, 0:D_MODEL]) * pa
              + _sigmoid(gates[:, D_MODEL:2 * D_MODEL]) * _dot(yb, wpb_ref[...]))
    x1_ref[...] = x + _dot(merged.astype(BF16), wo_ref[...])

    @pl.when(s == pl.num_programs(1) - 1)
    def _emit_state():
        klast_ref[...] = krot[tile - CHUNK:tile]
        vlast_ref[...] = vval[tile - CHUNK:tile]
        gmlast_ref[...] = vg[tile - CHUNK:tile]


def _const_spec(shape):
    nd = len(shape)
    return pl.BlockSpec(shape, lambda *_: (0,) * nd, pipeline_mode=pl.Buffered(1))


def _prompt_mixer(x, sinks, inv_row, g_mix, w_in, ln_g, ln_b, w_s, bs_full, w_pa, w_pb, w_o):
    nb, seq, _ = x.shape
    tile = PROMPT_TILE
    grid = (nb, seq // tile)
    row_spec = pl.BlockSpec((None, tile, D_MODEL), lambda b, s: (b, s, 0))
    last = lambda w: pl.BlockSpec((None, CHUNK, w), lambda b, s: (b, 0, 0))
    return pl.pallas_call(
        functools.partial(_prompt_mixer_kernel, tile=tile),
        grid=grid,
        in_specs=[
            pl.BlockSpec(memory_space=pltpu.SMEM),
            row_spec,
            _const_spec((1, LANES)),
            _const_spec((1, D_MODEL)),
            _const_spec((D_MODEL, IN_WIDTH)),
            _const_spec((1, GM_WIDTH)),
            _const_spec((1, GM_WIDTH)),
            _const_spec((GM_HEADS, CHUNK, CHUNK)),
            _const_spec((CHUNK, GM_WIDTH)),
            _const_spec((GM_WIDTH, D_MODEL)),
            _const_spec((Q_WIDTH, D_MODEL)),
            _const_spec((D_MODEL, D_MODEL)),
        ],
        out_specs=[row_spec, last(KV_WIDTH), last(KV_WIDTH), last(GM_WIDTH)],
        out_shape=[
            jax.ShapeDtypeStruct((nb, seq, D_MODEL), F32),
            jax.ShapeDtypeStruct((nb, CHUNK, KV_WIDTH), F32),
            jax.ShapeDtypeStruct((nb, CHUNK, KV_WIDTH), F32),
            jax.ShapeDtypeStruct((nb, CHUNK, GM_WIDTH), F32),
        ],
        scratch_shapes=[
            pltpu.VMEM((SEQ, LANES), F32),
            pltpu.VMEM((SEQ, LANES), F32),
            pltpu.VMEM((GM_HEADS // 2, CHUNK, 2 * CHUNK), BF16),
            pltpu.VMEM((CHUNK, KV_WIDTH), F32),
            pltpu.VMEM((CHUNK, KV_WIDTH), F32),
            pltpu.VMEM((tile, D_MODEL), BF16),
        ],
        compiler_params=pltpu.CompilerParams(
            dimension_semantics=("arbitrary", "arbitrary"), vmem_limit_bytes=VMEM_LIMIT),
        name="prompt_mixer",
    )(sinks, x, inv_row, g_mix, w_in, ln_g, ln_b, w_s, bs_full, w_pa, w_pb, w_o)


def _prompt_ffn_kernel(x_ref, g_ref, wup_ref, cw_ref, cb_ref, wdn_ref, gf_ref,
                       y_ref, convlast_ref, zbuf_ref, gbuf_ref, ybuf_ref, h_ref, *, tile):
    s = pl.program_id(1)
    pad = SUBLANES
    q = tile // ROW_PHASES
    nslab_half = D_FF // LANES

    @pl.when(s == 0)
    def _reset_carry():
        zbuf_ref[:, 0:pad, :] = jnp.zeros((2 * nslab_half, pad, LANES), F32)

    x = x_ref[...]
    h_ref[...] = _rms(x, g_ref[...]).astype(BF16)

    def up_project(j):
        for half in range(2):
            n0 = half * D_FF + j * FF_CHUNK
            z = _dot(h_ref[...], wup_ref[:, n0:n0 + FF_CHUNK])
            for jj in range(FF_CHUNK // LANES):
                zbuf_ref[n0 // LANES + jj, pad:pad + tile, :] = z[:, jj * LANES:(jj + 1) * LANES]

    def conv(slab):
        cs = slice(slab * LANES, (slab + 1) * LANES)
        rows = [zbuf_ref[slab, pl.ds(pad - (CONV_W - 1) + m, q, stride=ROW_PHASES), :]
                for m in range(ROW_PHASES + CONV_W - 1)]
        taps = [cw_ref[t:t + 1, cs] for t in range(CONV_W)]
        bias = cb_ref[:, cs]
        return [bias + taps[0] * rows[r] + taps[1] * rows[r + 1] + taps[2] * rows[r + 2]
                for r in range(ROW_PHASES)]

    def gate_and_project(j, yperm):
        for ja in range(j * (FF_CHUNK // LANES), (j + 1) * (FF_CHUNK // LANES)):
            ca = conv(ja)
            cb = conv(ja + nslab_half)
            for r in range(ROW_PHASES):
                gbuf_ref[r * q:(r + 1) * q, ja * LANES:(ja + 1) * LANES] = (_gelu(ca[r]) * cb[r]).astype(BF16)
        cs = slice(j * FF_CHUNK, (j + 1) * FF_CHUNK)
        part = _dot(gbuf_ref[:, cs], wdn_ref[cs, :])
        return part if yperm is None else yperm + part

    nchunk = D_FF // FF_CHUNK
    yperm = None
    for j in range(nchunk + FFN_LAG):
        if j < nchunk:
            up_project(j)
        if j >= FFN_LAG:
            yperm = gate_and_project(j - FFN_LAG, yperm)

    for j in range(D_MODEL // LANES):
        for r in range(ROW_PHASES):
            ybuf_ref[j, pl.ds(r, q, stride=ROW_PHASES), :] = yperm[r * q:(r + 1) * q, j * LANES:(j + 1) * LANES]
    y = jnp.concatenate([ybuf_ref[j] for j in range(D_MODEL // LANES)], axis=1)
    y_ref[...] = _rms(x + y, gf_ref[...])

    @pl.when(s == pl.num_programs(1) - 1)
    def _emit_state():
        for slab in range(2 * nslab_half):
            convlast_ref[:, slab * LANES:(slab + 1) * LANES] = (
                zbuf_ref[slab, pad + tile - (CONV_W - 1):pad + tile, :])

    zbuf_ref[:, 0:pad, :] = zbuf_ref[:, tile:tile + pad, :]


def _prompt_ffn(x1, g_ffn, w_up, conv_w, conv_b, w_down, g_final):
    nb, seq, _ = x1.shape
    tile = PROMPT_TILE
    row_spec = pl.BlockSpec((None, tile, D_MODEL), lambda b, s: (b, s, 0))
    return pl.pallas_call(
        functools.partial(_prompt_ffn_kernel, tile=tile),
        grid=(nb, seq // tile),
        in_specs=[
            row_spec,
            _const_spec((1, D_MODEL)),
            _const_spec((D_MODEL, 2 * D_FF)),
            _const_spec((CONV_W, 2 * D_FF)),
            _const_spec((1, 2 * D_FF)),
            _const_spec((D_FF, D_MODEL)),
            _const_spec((1, D_MODEL)),
        ],
        out_specs=[row_spec, pl.BlockSpec((None, CONV_W - 1, 2 * D_FF), lambda b, s: (b, 0, 0))],
        out_shape=[
            jax.ShapeDtypeStruct((nb, seq, D_MODEL), F32),
            jax.ShapeDtypeStruct((nb, CONV_W - 1, 2 * D_FF), F32),
        ],
        scratch_shapes=[
            pltpu.VMEM((2 * D_FF // LANES, tile + SUBLANES, LANES), F32),
            pltpu.VMEM((tile, D_FF), BF16),
            pltpu.VMEM((D_MODEL // LANES, tile, LANES), F32),
            pltpu.VMEM((tile, D_MODEL), BF16),
        ],
        compiler_params=pltpu.CompilerParams(
            dimension_semantics=("arbitrary", "arbitrary"), vmem_limit_bytes=VMEM_LIMIT),
        name="prompt_ffn",
    )(x1, g_ffn, w_up, conv_w, conv_b, w_down, g_final)


def _sample_mixer_kernel(sinks_ref, x_ref, ckt_ref, cvt_ref, inv_ref, g_ref, win_ref, lng_ref, lnb_ref,
                         ws0_ref, bs0_ref, wpa_ref, wpb_ref, wo_ref,
                         x1_ref, gm_ref, cktn_ref, cvtn_ref,
                         h_ref, ya_ref, qz_ref, kn_ref, vn_ref, knt_ref, vnt_ref, o_ref, *, chunk):
    i = pl.program_id(0)
    n = x_ref.shape[0]
    lane = lax.broadcasted_iota(jnp.int32, (n, LANES), 1)
    low = lane < HEAD_DIM

    @pl.when(i == 0)
    def _project_in():
        first_half = (lane % HEAD_DIM) < HEAD_DIM // 2
        h = _rms(x_ref[...], g_ref[...]).astype(BF16)
        h_ref[...] = h
        a = _gelu(_dot(h, win_ref[:, 0:OFF_Q]))
        vg = _layer_norm(a[:, GM_WIDTH:OFF_Q], lng_ref[...], lnb_ref[...])
        gm_ref[...] = vg
        ya_ref[...] = (a[:, 0:GM_WIDTH] * (vg * ws0_ref[...] + bs0_ref[...])).astype(BF16)

        zqkv = _dot(h, win_ref[:, OFF_Q:OFF_GA])
        pos = jnp.full((n, LANES), float(PAST_LEN), F32)
        cos, sin_signed = _rope_tables(pos, inv_ref[...], (n, LANES))
        knew = _rope(zqkv[:, Q_WIDTH:Q_WIDTH + KV_WIDTH], cos, sin_signed, first_half)
        vnew = zqkv[:, Q_WIDTH + KV_WIDTH:Q_WIDTH + 2 * KV_WIDTH]
        kn_ref[...] = knew
        vn_ref[...] = vnew
        knt_ref[...] = knew.T
        vnt_ref[...] = vnew.T
        for j in range(Q_WIDTH // LANES):
            qc = _rope(zqkv[:, j * LANES:(j + 1) * LANES], cos, sin_signed, first_half) * (HEAD_DIM ** -0.5)
            qsw = pltpu.roll(qc, HEAD_DIM, 1)
            g = j // 2
            even = jnp.where(low, qc, 0.0) if g == 0 else jnp.where(low, 0.0, qsw)
            odd = jnp.where(low, qsw, 0.0) if g == 0 else jnp.where(low, 0.0, qc)
            qz_ref[pl.ds(2 * j, n, stride=N_HEADS), :] = even
            qz_ref[pl.ds(2 * j + 1, n, stride=N_HEADS), :] = odd

    hrow = lax.broadcasted_iota(jnp.int32, (N_HEADS, 1), 0)
    sink_col = jnp.zeros((N_HEADS, 1), F32)
    for hd in range(N_HEADS):
        sink_col = jnp.where(hrow == hd, sinks_ref[hd], sink_col)
    head_low = lax.broadcasted_iota(jnp.int32, (N_HEADS, LANES), 0) < N_HEADS // KV_HEADS
    lane_low = lax.broadcasted_iota(jnp.int32, (N_HEADS, LANES), 1) < HEAD_DIM
    own_half = head_low == lane_low
    last_key = lax.broadcasted_iota(jnp.int32, (KV_WIDTH, CHUNK), 1) == CHUNK - 1

    def body(bi, carry):
        b = i * chunk + bi
        qz = qz_ref[pl.ds(pl.multiple_of(b * N_HEADS, N_HEADS), N_HEADS), :]
        kt = ckt_ref[bi]
        vt = cvt_ref[bi]
        kn = kn_ref[pl.ds(b, 1), :]
        vn = vn_ref[pl.ds(b, 1), :]
        sc = _dot(qz.astype(BF16), kt.astype(BF16))
        sn = jnp.sum(qz * kn, axis=-1, keepdims=True)
        m = jnp.maximum(jnp.maximum(jnp.max(sc, axis=-1, keepdims=True), sn), sink_col)
        p = jnp.exp(sc - m)
        pn = jnp.exp(sn - m)
        lsum = jnp.sum(p, axis=-1, keepdims=True) + pn + jnp.exp(sink_col - m)
        o = (_dot_nt(p.astype(BF16), vt.astype(BF16)) + pn * vn) / lsum
        o_ref[pl.ds(pl.multiple_of(b * N_HEADS, N_HEADS), N_HEADS), :] = jnp.where(own_half, o, 0.0)
        shift = lax.rem(n + CHUNK - 1 - b, n)
        cktn_ref[bi] = jnp.where(last_key, pltpu.roll(knt_ref[...], shift, 1), pltpu.roll(kt, CHUNK - 1, 1))
        cvtn_ref[bi] = jnp.where(last_key, pltpu.roll(vnt_ref[...], shift, 1), pltpu.roll(vt, CHUNK - 1, 1))
        return carry

    lax.fori_loop(0, chunk, body, 0, unroll=SAMPLE_UNROLL)

    @pl.when(i == pl.num_programs(0) - 1)
    def _project_out():
        cols = []
        for j in range(Q_WIDTH // LANES):
            g = j // 2
            oe = o_ref[pl.ds(2 * j, n, stride=N_HEADS), :]
            oo = o_ref[pl.ds(2 * j + 1, n, stride=N_HEADS), :]
            if g == 0:
                cols.append(jnp.where(low, oe, pltpu.roll(oo, HEAD_DIM, 1)))
            else:
                cols.append(jnp.where(low, pltpu.roll(oe, HEAD_DIM, 1), oo))
        yb = jnp.concatenate(cols, axis=1).astype(BF16)
        gates = _dot(h_ref[...], win_ref[:, OFF_GA:IN_WIDTH])
        merged = (_sigmoid(gates[:, 0:D_MODEL]) * _dot(ya_ref[...], wpa_ref[...])
                  + _sigmoid(gates[:, D_MODEL:2 * D_MODEL]) * _dot(yb, wpb_ref[...]))
        x1_ref[...] = x_ref[...] + _dot(merged.astype(BF16), wo_ref[...])


def _sample_mixer(x, ckt, cvt, sinks, inv_row, g_mix, w_in, ln_g, ln_b, ws0, bs0, w_pa, w_pb, w_o):
    n = x.shape[0]
    chunk = SAMPLE_CHUNK
    full = lambda w: pl.BlockSpec((n, w), lambda i: (0, 0))
    cache = pl.BlockSpec((chunk, KV_WIDTH, CHUNK), lambda i: (i, 0, 0))
    return pl.pallas_call(
        functools.partial(_sample_mixer_kernel, chunk=chunk),
        grid=(n // chunk,),
        in_specs=[
            pl.BlockSpec(memory_space=pltpu.SMEM),
            full(D_MODEL), cache, cache,
            _const_spec((1, LANES)),
            _const_spec((1, D_MODEL)),
            _const_spec((D_MODEL, IN_WIDTH)),
            _const_spec((1, GM_WIDTH)),
            _const_spec((1, GM_WIDTH)),
            _const_spec((1, GM_WIDTH)),
            _const_spec((1, GM_WIDTH)),
            _const_spec((GM_WIDTH, D_MODEL)),
            _const_spec((Q_WIDTH, D_MODEL)),
            _const_spec((D_MODEL, D_MODEL)),
        ],
        out_specs=[full(D_MODEL), full(GM_WIDTH), cache, cache],
        out_shape=[
            jax.ShapeDtypeStruct((n, D_MODEL), F32),
            jax.ShapeDtypeStruct((n, GM_WIDTH), F32),
            jax.ShapeDtypeStruct((n, KV_WIDTH, CHUNK), F32),
            jax.ShapeDtypeStruct((n, KV_WIDTH, CHUNK), F32),
        ],
        scratch_shapes=[
            pltpu.VMEM((n, D_MODEL), BF16),
            pltpu.VMEM((n, GM_WIDTH), BF16),
            pltpu.VMEM((N_HEADS * n, LANES), F32),
            pltpu.VMEM((n, KV_WIDTH), F32),
            pltpu.VMEM((n, KV_WIDTH), F32),
            pltpu.VMEM((KV_WIDTH, n), F32),
            pltpu.VMEM((KV_WIDTH, n), F32),
            pltpu.VMEM((N_HEADS * n, LANES), F32),
        ],
        compiler_params=pltpu.CompilerParams(
            dimension_semantics=("arbitrary",), vmem_limit_bytes=VMEM_LIMIT),
        name="sample_mixer",
    )(sinks, x, ckt, cvt, inv_row, g_mix, w_in, ln_g, ln_b, ws0, bs0, w_pa, w_pb, w_o)


def _sample_ffn_kernel(x_ref, g_ref, wua_ref, wub_ref, cwa_ref, cwb_ref, cba_ref, cbb_ref,
                       p0a_ref, p0b_ref, p1a_ref, p1b_ref, wdn_ref, gf_ref,
                       y_ref, za_ref, zb_ref, acc_ref):
    j = pl.program_id(0)
    x = x_ref[...]
    h = _rms(x, g_ref[...]).astype(BF16)
    za = _dot(h, wua_ref[...])
    zb = _dot(h, wub_ref[...])
    za_ref[...] = za
    zb_ref[...] = zb
    ca = cba_ref[...] + cwa_ref[0:1, :] * p0a_ref[...] + cwa_ref[1:2, :] * p1a_ref[...] + cwa_ref[2:3, :] * za
    cb = cbb_ref[...] + cwb_ref[0:1, :] * p0b_ref[...] + cwb_ref[1:2, :] * p1b_ref[...] + cwb_ref[2:3, :] * zb
    part = _dot((_gelu(ca) * cb).astype(BF16), wdn_ref[...])

    @pl.when(j == 0)
    def _first():
        acc_ref[...] = part

    @pl.when(j > 0)
    def _rest():
        acc_ref[...] += part

    @pl.when(j == pl.num_programs(0) - 1)
    def _finish():
        y_ref[...] = _rms(x + acc_ref[...], gf_ref[...])


def _sample_ffn(x1, state2d, g_ffn, w_up, conv_w, conv_b, w_down, g_final):
    n = x1.shape[0]
    nchunk = D_FF // FF_CHUNK
    full = lambda w: pl.BlockSpec((n, w), lambda j: (0, 0))
    col = lambda rows, off: pl.BlockSpec((rows, FF_CHUNK), lambda j, off=off: (0, j + off))
    return pl.pallas_call(
        _sample_ffn_kernel,
        grid=(nchunk,),
        in_specs=[
            full(D_MODEL),
            pl.BlockSpec((1, D_MODEL), lambda j: (0, 0)),
            col(D_MODEL, 0), col(D_MODEL, nchunk),
            col(CONV_W, 0), col(CONV_W, nchunk),
            col(1, 0), col(1, nchunk),
            col(n, 0), col(n, nchunk), col(n, 2 * nchunk), col(n, 3 * nchunk),
            pl.BlockSpec((FF_CHUNK, D_MODEL), lambda j: (j, 0)),
            pl.BlockSpec((1, D_MODEL), lambda j: (0, 0)),
        ],
        out_specs=[full(D_MODEL), col(n, 0), col(n, 0)],
        out_shape=[
            jax.ShapeDtypeStruct((n, D_MODEL), F32),
            jax.ShapeDtypeStruct((n, D_FF), F32),
            jax.ShapeDtypeStruct((n, D_FF), F32),
        ],
        scratch_shapes=[pltpu.VMEM((n, D_MODEL), F32)],
        compiler_params=pltpu.CompilerParams(
            dimension_semantics=("arbitrary",), vmem_limit_bytes=VMEM_LIMIT),
        name="sample_ffn",
    )(x1, g_ffn, w_up, w_up, conv_w, conv_w, conv_b, conv_b,
      state2d, state2d, state2d, state2d, w_down, g_final)


def kernel(x_prompt, x_sample, cache_swa_k, cache_swa_v, state_ffn_conv, g_mix, w_in, ln_v_g, ln_v_b,
           w_s, b_s, sinks, w_pa, w_pb, w_o, g_ffn, w_up, conv_w, conv_b, w_down, g_final):
    depth = g_mix.shape[0]
    assert depth == 1
    nb = x_prompt.shape[0]
    nd = x_sample.shape[0]
    half = HEAD_DIM // 2
    inv = ROPE_THETA ** (-jnp.arange(half, dtype=F32) / half)
    inv_row = jnp.tile(inv, LANES // half)[None, :]

    l = 0
    row = lambda v: v[l][None, :]
    w_in_b = w_in[l].astype(BF16)
    w_pa_b = w_pa[l].astype(BF16)
    w_pb_b = w_pb[l].astype(BF16)
    w_o_b = w_o[l].astype(BF16)
    w_up_b = w_up[l].astype(BF16)
    w_dn_b = w_down[l].astype(BF16)
    bs_full = jnp.repeat(b_s[l].T, HEAD_DIM, axis=1)
    ws0 = jnp.repeat(w_s[l][:, 0, 0], HEAD_DIM)[None, :]
    bs0 = bs_full[0:1]

    x1p, kp, vp, gmp = _prompt_mixer(x_prompt, sinks[l], inv_row, row(g_mix), w_in_b, row(ln_v_g), row(ln_v_b),
                                     w_s[l], bs_full, w_pa_b, w_pb_b, w_o_b)
    yp, convp = _prompt_ffn(x1p, row(g_ffn), w_up_b, conv_w[l], row(conv_b), w_dn_b, g_final[None, :])

    xs = x_sample.reshape(nd, D_MODEL)
    to_feature_major = lambda c: jnp.transpose(c, (0, 2, 3, 1)).reshape(nd, KV_WIDTH, CHUNK)
    from_feature_major = lambda c: jnp.transpose(c.reshape(nd, KV_HEADS, HEAD_DIM, CHUNK), (0, 3, 1, 2))[None]
    x1s, gms, cktn, cvtn = _sample_mixer(xs, to_feature_major(cache_swa_k[l]), to_feature_major(cache_swa_v[l]),
                                         sinks[l], inv_row, row(g_mix), w_in_b, row(ln_v_g),
                                         row(ln_v_b), ws0, bs0, w_pa_b, w_pb_b, w_o_b)
    state = state_ffn_conv[l]
    ys, za, zb = _sample_ffn(x1s, state.reshape(nd, (CONV_W - 1) * 2 * D_FF), row(g_ffn), w_up_b,
                             conv_w[l], row(conv_b), w_dn_b, g_final[None, :])
    conv_s = jnp.stack([state[:, 1, :], jnp.concatenate([za, zb], axis=1)], axis=1)

    return (yp,
            ys.reshape(nd, 1, D_MODEL),
            kp.reshape(1, nb, CHUNK, KV_HEADS, HEAD_DIM),
            vp.reshape(1, nb, CHUNK, KV_HEADS, HEAD_DIM),
            gmp.reshape(1, nb, CHUNK, GM_HEADS, GM_WIDTH // GM_HEADS),
            convp[None],
            from_feature_major(cktn),
            from_feature_major(cvtn),
            gms.reshape(1, nd, 1, GM_HEADS, GM_WIDTH // GM_HEADS),
            conv_s[None])
```

```python
import functools
import math

import jax
import jax.numpy as jnp
from jax import lax
from jax.experimental import pallas as pl
from jax.experimental.pallas import tpu as pltpu

F32 = jnp.float32
BF16 = jnp.bfloat16

D_MODEL = 1024
SEQ = 2048
PAST_LEN = 16384
CHUNK = 128
GM_HEADS = 8
GM_WIDTH = 512
N_HEADS = 8
KV_HEADS = 2
HEAD_DIM = 64
Q_WIDTH = N_HEADS * HEAD_DIM
KV_WIDTH = KV_HEADS * HEAD_DIM
D_FF = 3 * D_MODEL
CONV_W = 3
EPS = 1e-6
ROPE_THETA = 10000.0
LANES = 128
SUBLANES = 8
OFF_Q = 2 * GM_WIDTH
OFF_K = OFF_Q + Q_WIDTH
OFF_V = OFF_K + KV_WIDTH
OFF_GA = OFF_V + KV_WIDTH
OFF_GB = OFF_GA + D_MODEL
IN_WIDTH = OFF_GB + D_MODEL

PROMPT_TILE = 512
SAMPLE_CHUNK = 16
SAMPLE_UNROLL = 8
FF_CHUNK = 512
FFN_LAG = 4
ROW_PHASES = 4
VMEM_LIMIT = 52 * 1024 * 1024

_GELU_C0 = math.sqrt(2.0 / math.pi)
_GELU_C1 = _GELU_C0 * 0.044715


def _gelu(x):
    inner = x * (_GELU_C0 + _GELU_C1 * (x * x))
    hx = 0.5 * x
    return hx + hx * jnp.tanh(inner)


def _rms(x, g):
    ms = jnp.mean(x * x, axis=-1, keepdims=True)
    return x * lax.rsqrt(ms + EPS) * g


def _layer_norm(x, g, b):
    mu = jnp.mean(x, axis=-1, keepdims=True)
    xc = x - mu
    var = jnp.mean(xc * xc, axis=-1, keepdims=True)
    return xc * lax.rsqrt(var + EPS) * g + b


def _sigmoid(x):
    return 1.0 / (1.0 + jnp.exp(-x))


def _dot(a, b):
    return jnp.dot(a, b, preferred_element_type=F32)


def _dot_nt(a, b):
    return lax.dot_general(a, b, (((1,), (1,)), ((), ())), preferred_element_type=F32)


def _rope(xc, cos, sin_signed, first_half):
    partner = jnp.where(first_half, pltpu.roll(xc, LANES - 32, 1), pltpu.roll(xc, 32, 1))
    return xc * cos + partner * sin_signed


def _rope_tables(pos_f32, inv_row, shape):
    lane = lax.broadcasted_iota(jnp.int32, shape, 1)
    ang = pos_f32 * inv_row
    sin = jnp.sin(ang)
    return jnp.cos(ang), jnp.where((lane % HEAD_DIM) < HEAD_DIM // 2, -sin, sin)


def _prompt_mixer_kernel(sinks_ref, x_ref, inv_ref, g_ref, win_ref, lng_ref, lnb_ref, ws_ref, bs_ref,
                         wpa_ref, wpb_ref, wo_ref,
                         x1_ref, klast_ref, vlast_ref, gmlast_ref,
                         cos_ref, sin_ref, wcat_ref, kprev_ref, vprev_ref, h_ref, *, tile):
    b = pl.program_id(0)
    s = pl.program_id(1)
    nblk = tile // CHUNK

    @pl.when((b == 0) & (s == 0))
    def _init_tables():
        def body(i, carry):
            r0 = pl.multiple_of(i * CHUNK, CHUNK)
            pos = (lax.broadcasted_iota(jnp.int32, (CHUNK, LANES), 0) + r0).astype(F32)
            cos, sin_signed = _rope_tables(pos, inv_ref[...], (CHUNK, LANES))
            cos_ref[pl.ds(r0, CHUNK), :] = cos
            sin_ref[pl.ds(r0, CHUNK), :] = sin_signed
            return carry
        lax.fori_loop(0, SEQ // CHUNK, body, 0)
        row = lax.broadcasted_iota(jnp.int32, (CHUNK, CHUNK), 0)
        col = lax.broadcasted_iota(jnp.int32, (CHUNK, CHUNK), 1)
        causal = col <= row
        for j in range(GM_HEADS // 2):
            wcat_ref[j, :, 0:CHUNK] = jnp.where(causal, ws_ref[2 * j], 0.0).astype(BF16)
            wcat_ref[j, :, CHUNK:2 * CHUNK] = jnp.where(causal, ws_ref[2 * j + 1], 0.0).astype(BF16)

    @pl.when(s == 0)
    def _reset_carry():
        kprev_ref[...] = jnp.zeros_like(kprev_ref)
        vprev_ref[...] = jnp.zeros_like(vprev_ref)

    lane = lax.broadcasted_iota(jnp.int32, (CHUNK, LANES), 1)
    low = lane < HEAD_DIM
    low2 = lax.broadcasted_iota(jnp.int32, (2 * CHUNK, LANES), 1) < HEAD_DIM
    first_half_t = (lax.broadcasted_iota(jnp.int32, (tile, LANES), 1) % HEAD_DIM) < HEAD_DIM // 2
    row = lax.broadcasted_iota(jnp.int32, (CHUNK, CHUNK), 0)
    col = lax.broadcasted_iota(jnp.int32, (CHUNK, CHUNK), 1)
    cur_mask = col <= row
    prev_live = (col - row) >= jnp.where(s > 0, 0, 2 * CHUNK)
    prev_inner = col >= row

    x = x_ref[...]
    h_ref[...] = _rms(x, g_ref[...]).astype(BF16)

    zqkv = _dot(h_ref[...], win_ref[:, OFF_Q:OFF_GA])
    zuv = _dot(h_ref[...], win_ref[:, 0:OFF_Q])

    r0 = pl.multiple_of(s * tile, tile)
    cos = cos_ref[pl.ds(r0, tile), :]
    sin_signed = sin_ref[pl.ds(r0, tile), :]
    qcols = [(_rope(zqkv[:, j * LANES:(j + 1) * LANES], cos, sin_signed, first_half_t)
              * (HEAD_DIM ** -0.5)).astype(BF16) for j in range(Q_WIDTH // LANES)]
    krot = _rope(zqkv[:, Q_WIDTH:Q_WIDTH + KV_WIDTH], cos, sin_signed, first_half_t)
    vval = zqkv[:, Q_WIDTH + KV_WIDTH:Q_WIDTH + 2 * KV_WIDTH]

    kp = kprev_ref[...]
    vp = vprev_ref[...]
    scores = {}
    vblks = {}
    for i in range(nblk):
        r = slice(i * CHUNK, (i + 1) * CHUNK)
        kk = jnp.concatenate([kp, krot[r]], axis=0)
        vv = jnp.concatenate([vp, vval[r]], axis=0)
        kk_sw = pltpu.roll(kk, HEAD_DIM, 1)
        vv_sw = pltpu.roll(vv, HEAD_DIM, 1)
        for g in range(KV_HEADS):
            k_lo = jnp.where(low2, kk if g == 0 else kk_sw, 0.0).astype(BF16)
            k_hi = jnp.where(low2, 0.0, kk_sw if g == 0 else kk).astype(BF16)
            v_lo = jnp.where(low2, vv if g == 0 else vv_sw, 0.0).astype(BF16)
            v_hi = jnp.where(low2, 0.0, vv_sw if g == 0 else vv).astype(BF16)
            krhs = jnp.concatenate([k_lo, k_hi], axis=0)
            vblks[i, g] = jnp.concatenate([v_lo, v_hi], axis=0)
            qstack = jnp.concatenate([qcols[2 * g][r], qcols[2 * g + 1][r]], axis=0)
            scores[i, g] = _dot_nt(qstack, krhs)
        kp = krot[r]
        vp = vval[r]
    kprev_ref[...] = kp
    vprev_ref[...] = vp

    gates = _dot(h_ref[...], win_ref[:, OFF_GA:IN_WIDTH])

    a = _gelu(zuv)
    u = a[:, 0:GM_WIDTH]
    vg = _layer_norm(a[:, GM_WIDTH:OFF_Q], lng_ref[...], lnb_ref[...])
    ya_blocks = []
    for c in range(nblk):
        r = slice(c * CHUNK, (c + 1) * CHUNK)
        cols = []
        for j in range(GM_HEADS // 2):
            vcol = vg[r, j * LANES:(j + 1) * LANES]
            rhs = jnp.concatenate([jnp.where(low, vcol, 0.0), jnp.where(low, 0.0, vcol)], axis=0).astype(BF16)
            cols.append(_dot(wcat_ref[j], rhs))
        mix = jnp.concatenate(cols, axis=1) + bs_ref[...]
        ya_blocks.append((u[r] * mix).astype(BF16))
    ya = jnp.concatenate(ya_blocks, axis=0)
    pa = _dot(ya, wpa_ref[...])

    yb_blocks = []
    for i in range(nblk):
        pmask = prev_live if i == 0 else prev_inner
        cols_out = []
        for g in range(KV_HEADS):
            sc = scores[i, g]
            p_rows = []
            rinv_rows = []
            for jj in range(2):
                ps = []
                ls = []
                for par in range(2):
                    hd = 4 * g + 2 * jj + par
                    sh = sc[jj * CHUNK:(jj + 1) * CHUNK, par * 2 * CHUNK:(par + 1) * 2 * CHUNK]
                    sp = jnp.where(pmask, sh[:, 0:CHUNK], -jnp.inf)
                    scur = jnp.where(cur_mask, sh[:, CHUNK:2 * CHUNK], -jnp.inf)
                    sink = sinks_ref[hd]
                    m = jnp.maximum(jnp.max(jnp.maximum(sp, scur), axis=-1, keepdims=True), sink)
                    pp = jnp.exp(sp - m)
                    pc = jnp.exp(scur - m)
                    lsum = jnp.sum(pp + pc, axis=-1, keepdims=True) + jnp.exp(sink - m)
                    ps.append(pp.astype(BF16))
                    ps.append(pc.astype(BF16))
                    ls.append(lsum)
                p_rows.append(jnp.concatenate(ps, axis=1))
                rinv_rows.append(jnp.where(low, 1.0 / ls[0], 1.0 / ls[1]))
            o = _dot(jnp.concatenate(p_rows, axis=0), vblks[i, g])
            cols_out.append(o[0:CHUNK] * rinv_rows[0])
            cols_out.append(o[CHUNK:2 * CHUNK] * rinv_rows[1])
        yb_blocks.append(jnp.concatenate(cols_out, axis=1).astype(BF16))
    yb = jnp.concatenate(yb_blocks, axis=0)

    merged = (_sigmoid(gates[:, 0:D_MODEL]) * pa
              + _sigmoid(gates[:, D_MODEL:2 * D_MODEL]) * _dot(yb, wpb_ref[...]))
    x1_ref[...] = x + _dot(merged.astype(BF16), wo_ref[...])

    @pl.when(s == pl.num_programs(1) - 1)
    def _emit_state():
        klast_ref[...] = krot[tile - CHUNK:tile].T
        vlast_ref[...] = vval[tile - CHUNK:tile].T
        gmlast_ref[...] = vg[tile - CHUNK:tile].T


def _const_spec(shape):
    nd = len(shape)
    return pl.BlockSpec(shape, lambda *_: (0,) * nd, pipeline_mode=pl.Buffered(1))


def _prompt_mixer(x, sinks, inv_row, g_mix, w_in, ln_g, ln_b, w_s, bs_full, w_pa, w_pb, w_o):
    nb, seq, _ = x.shape
    tile = PROMPT_TILE
    grid = (nb, seq // tile)
    row_spec = pl.BlockSpec((None, tile, D_MODEL), lambda b, s: (b, s, 0))
    last = lambda w: pl.BlockSpec((None, w, CHUNK), lambda b, s: (b, 0, 0))
    return pl.pallas_call(
        functools.partial(_prompt_mixer_kernel, tile=tile),
        grid=grid,
        in_specs=[
            pl.BlockSpec(memory_space=pltpu.SMEM),
            row_spec,
            _const_spec((1, LANES)),
            _const_spec((1, D_MODEL)),
            _const_spec((D_MODEL, IN_WIDTH)),
            _const_spec((1, GM_WIDTH)),
            _const_spec((1, GM_WIDTH)),
            _const_spec((GM_HEADS, CHUNK, CHUNK)),
            _const_spec((CHUNK, GM_WIDTH)),
            _const_spec((GM_WIDTH, D_MODEL)),
            _const_spec((Q_WIDTH, D_MODEL)),
            _const_spec((D_MODEL, D_MODEL)),
        ],
        out_specs=[row_spec, last(KV_WIDTH), last(KV_WIDTH), last(GM_WIDTH)],
        out_shape=[
            jax.ShapeDtypeStruct((nb, seq, D_MODEL), F32),
            jax.ShapeDtypeStruct((nb, KV_WIDTH, CHUNK), F32),
            jax.ShapeDtypeStruct((nb, KV_WIDTH, CHUNK), F32),
            jax.ShapeDtypeStruct((nb, GM_WIDTH, CHUNK), F32),
        ],
        scratch_shapes=[
            pltpu.VMEM((SEQ, LANES), F32),
            pltpu.VMEM((SEQ, LANES), F32),
            pltpu.VMEM((GM_HEADS // 2, CHUNK, 2 * CHUNK), BF16),
            pltpu.VMEM((CHUNK, KV_WIDTH), F32),
            pltpu.VMEM((CHUNK, KV_WIDTH), F32),
            pltpu.VMEM((tile, D_MODEL), BF16),
        ],
        compiler_params=pltpu.CompilerParams(
            dimension_semantics=("arbitrary", "arbitrary"), vmem_limit_bytes=VMEM_LIMIT),
        name="prompt_mixer",
    )(sinks, x, inv_row, g_mix, w_in, ln_g, ln_b, w_s, bs_full, w_pa, w_pb, w_o)


def _prompt_ffn_kernel(x_ref, g_ref, wupa_ref, wupb_ref, cw_ref, cb_ref, wdn_ref, gf_ref,
                       y_ref, convlast_ref, zbuf_ref, gbuf_ref, ybuf_ref, h_ref, *, tile):
    s = pl.program_id(1)
    pad = SUBLANES
    q = tile // ROW_PHASES
    nslab_half = D_FF // LANES

    @pl.when(s == 0)
    def _reset_carry():
        zbuf_ref[:, 0:pad, :] = jnp.zeros((2 * nslab_half, pad, LANES), F32)

    x = x_ref[...]
    h_ref[...] = _rms(x, g_ref[...]).astype(BF16)

    def up_project(j):
        for half in range(2):
            n0 = half * D_FF + j * FF_CHUNK
            w_half = wupb_ref if half else wupa_ref
            z = _dot(h_ref[...], w_half[:, j * FF_CHUNK:(j + 1) * FF_CHUNK])
            for jj in range(FF_CHUNK // LANES):
                zbuf_ref[n0 // LANES + jj, pad:pad + tile, :] = z[:, jj * LANES:(jj + 1) * LANES]

    def conv(slab):
        cs = slice(slab * LANES, (slab + 1) * LANES)
        rows = [zbuf_ref[slab, pl.ds(pad - (CONV_W - 1) + m, q, stride=ROW_PHASES), :]
                for m in range(ROW_PHASES + CONV_W - 1)]
        taps = [cw_ref[t:t + 1, cs] for t in range(CONV_W)]
        bias = cb_ref[:, cs]
        return [bias + taps[0] * rows[r] + taps[1] * rows[r + 1] + taps[2] * rows[r + 2]
                for r in range(ROW_PHASES)]

    def gate_and_project(j, yperm):
        for ja in range(j * (FF_CHUNK // LANES), (j + 1) * (FF_CHUNK // LANES)):
            ca = conv(ja)
            cb = conv(ja + nslab_half)
            for r in range(ROW_PHASES):
                gbuf_ref[r * q:(r + 1) * q, ja * LANES:(ja + 1) * LANES] = (_gelu(ca[r]) * cb[r]).astype(BF16)
        cs = slice(j * FF_CHUNK, (j + 1) * FF_CHUNK)
        part = _dot(gbuf_ref[:, cs], wdn_ref[cs, :])
        return part if yperm is None else yperm + part

    nchunk = D_FF // FF_CHUNK
    yperm = None
    for j in range(nchunk + FFN_LAG):
        if j < nchunk:
            up_project(j)
        if j >= FFN_LAG:
            yperm = gate_and_project(j - FFN_LAG, yperm)

    for j in range(D_MODEL // LANES):
        for r in range(ROW_PHASES):
            ybuf_ref[j, pl.ds(r, q, stride=ROW_PHASES), :] = yperm[r * q:(r + 1) * q, j * LANES:(j + 1) * LANES]
    y = jnp.concatenate([ybuf_ref[j] for j in range(D_MODEL // LANES)], axis=1)
    y_ref[...] = _rms(x + y, gf_ref[...])

    @pl.when(s == pl.num_programs(1) - 1)
    def _emit_state():
        for slab in range(2 * nslab_half):
            convlast_ref[:, slab * LANES:(slab + 1) * LANES] = (
                zbuf_ref[slab, pad + tile - (CONV_W - 1):pad + tile, :])

    zbuf_ref[:, 0:pad, :] = zbuf_ref[:, tile:tile + pad, :]


def _prompt_ffn(x1, g_ffn, w_up_a, w_up_b, conv_w, conv_b, w_down, g_final):
    nb, seq, _ = x1.shape
    tile = PROMPT_TILE
    row_spec = pl.BlockSpec((None, tile, D_MODEL), lambda b, s: (b, s, 0))
    return pl.pallas_call(
        functools.partial(_prompt_ffn_kernel, tile=tile),
        grid=(nb, seq // tile),
        in_specs=[
            row_spec,
            _const_spec((1, D_MODEL)),
            _const_spec((D_MODEL, D_FF)),
            _const_spec((D_MODEL, D_FF)),
            _const_spec((CONV_W, 2 * D_FF)),
            _const_spec((1, 2 * D_FF)),
            _const_spec((D_FF, D_MODEL)),
            _const_spec((1, D_MODEL)),
        ],
        out_specs=[row_spec, pl.BlockSpec((None, CONV_W - 1, 2 * D_FF), lambda b, s: (b, 0, 0))],
        out_shape=[
            jax.ShapeDtypeStruct((nb, seq, D_MODEL), F32),
            jax.ShapeDtypeStruct((nb, CONV_W - 1, 2 * D_FF), F32),
        ],
        scratch_shapes=[
            pltpu.VMEM((2 * D_FF // LANES, tile + SUBLANES, LANES), F32),
            pltpu.VMEM((tile, D_FF), BF16),
            pltpu.VMEM((D_MODEL // LANES, tile, LANES), F32),
            pltpu.VMEM((tile, D_MODEL), BF16),
        ],
        compiler_params=pltpu.CompilerParams(
            dimension_semantics=("arbitrary", "arbitrary"), vmem_limit_bytes=VMEM_LIMIT),
        name="prompt_ffn",
    )(x1, g_ffn, w_up_a, w_up_b, conv_w, conv_b, w_down, g_final)


def _sample_mixer_kernel(sinks_ref, x_ref, ckt_ref, cvt_ref, inv_ref, g_ref, win_ref, lng_ref, lnb_ref,
                         ws0_ref, bs0_ref, wpa_ref, wpb_ref, wo_ref,
                         x1_ref, gm_ref, cktn_ref, cvtn_ref,
                         h_ref, ya_ref, qz_ref, kn_ref, vn_ref, knt_ref, vnt_ref, o_ref, *, chunk):
    i = pl.program_id(0)
    n = x_ref.shape[0]
    lane = lax.broadcasted_iota(jnp.int32, (n, LANES), 1)
    low = lane < HEAD_DIM

    @pl.when(i == 0)
    def _project_in():
        first_half = (lane % HEAD_DIM) < HEAD_DIM // 2
        h = _rms(x_ref[...], g_ref[...]).astype(BF16)
        h_ref[...] = h
        a = _gelu(_dot(h, win_ref[:, 0:OFF_Q]))
        vg = _layer_norm(a[:, GM_WIDTH:OFF_Q], lng_ref[...], lnb_ref[...])
        gm_ref[...] = vg
        ya_ref[...] = (a[:, 0:GM_WIDTH] * (vg * ws0_ref[...] + bs0_ref[...])).astype(BF16)

        zqkv = _dot(h, win_ref[:, OFF_Q:OFF_GA])
        pos = jnp.full((n, LANES), float(PAST_LEN), F32)
        cos, sin_signed = _rope_tables(pos, inv_ref[...], (n, LANES))
        knew = _rope(zqkv[:, Q_WIDTH:Q_WIDTH + KV_WIDTH], cos, sin_signed, first_half)
        vnew = zqkv[:, Q_WIDTH + KV_WIDTH:Q_WIDTH + 2 * KV_WIDTH]
        kn_ref[...] = knew
        vn_ref[...] = vnew
        knt_ref[...] = knew.T
        vnt_ref[...] = vnew.T
        for j in range(Q_WIDTH // LANES):
            qc = _rope(zqkv[:, j * LANES:(j + 1) * LANES], cos, sin_signed, first_half) * (HEAD_DIM ** -0.5)
            qsw = pltpu.roll(qc, HEAD_DIM, 1)
            g = j // 2
            even = jnp.where(low, qc, 0.0) if g == 0 else jnp.where(low, 0.0, qsw)
            odd = jnp.where(low, qsw, 0.0) if g == 0 else jnp.where(low, 0.0, qc)
            qz_ref[pl.ds(2 * j, n, stride=N_HEADS), :] = even
            qz_ref[pl.ds(2 * j + 1, n, stride=N_HEADS), :] = odd

    hrow = lax.broadcasted_iota(jnp.int32, (N_HEADS, 1), 0)
    sink_col = jnp.zeros((N_HEADS, 1), F32)
    for hd in range(N_HEADS):
        sink_col = jnp.where(hrow == hd, sinks_ref[hd], sink_col)
    head_low = lax.broadcasted_iota(jnp.int32, (N_HEADS, LANES), 0) < N_HEADS // KV_HEADS
    lane_low = lax.broadcasted_iota(jnp.int32, (N_HEADS, LANES), 1) < HEAD_DIM
    own_half = head_low == lane_low
    last_key = lax.broadcasted_iota(jnp.int32, (KV_WIDTH, CHUNK), 1) == CHUNK - 1

    def body(bi, carry):
        b = i * chunk + bi
        qz = qz_ref[pl.ds(pl.multiple_of(b * N_HEADS, N_HEADS), N_HEADS), :]
        kt = ckt_ref[bi]
        vt = cvt_ref[bi]
        kn = kn_ref[pl.ds(b, 1), :]
        vn = vn_ref[pl.ds(b, 1), :]
        sc = _dot(qz.astype(BF16), kt.astype(BF16))
        sn = jnp.sum(qz * kn, axis=-1, keepdims=True)
        m = jnp.maximum(jnp.maximum(jnp.max(sc, axis=-1, keepdims=True), sn), sink_col)
        p = jnp.exp(sc - m)
        pn = jnp.exp(sn - m)
        lsum = jnp.sum(p, axis=-1, keepdims=True) + pn + jnp.exp(sink_col - m)
        o = (_dot_nt(p.astype(BF16), vt.astype(BF16)) + pn * vn) / lsum
        o_ref[pl.ds(pl.multiple_of(b * N_HEADS, N_HEADS), N_HEADS), :] = jnp.where(own_half, o, 0.0)
        shift = lax.rem(n + CHUNK - 1 - b, n)
        cktn_ref[bi] = jnp.where(last_key, pltpu.roll(knt_ref[...], shift, 1), pltpu.roll(kt, CHUNK - 1, 1))
        cvtn_ref[bi] = jnp.where(last_key, pltpu.roll(vnt_ref[...], shift, 1), pltpu.roll(vt, CHUNK - 1, 1))
        return carry

    lax.fori_loop(0, chunk, body, 0, unroll=SAMPLE_UNROLL)

    @pl.when(i == pl.num_programs(0) - 1)
    def _project_out():
        cols = []
        for j in range(Q_WIDTH // LANES):
            g = j // 2
            oe = o_ref[pl.ds(2 * j, n, stride=N_HEADS), :]
            oo = o_ref[pl.ds(2 * j + 1, n, stride=N_HEADS), :]
            if g == 0:
                cols.append(jnp.where(low, oe, pltpu.roll(oo, HEAD_DIM, 1)))
            else:
                cols.append(jnp.where(low, pltpu.roll(oe, HEAD_DIM, 1), oo))
        yb = jnp.concatenate(cols, axis=1).astype(BF16)
        gates = _dot(h_ref[...], win_ref[:, OFF_GA:IN_WIDTH])
        merged = (_sigmoid(gates[:, 0:D_MODEL]) * _dot(ya_ref[...], wpa_ref[...])
                  + _sigmoid(gates[:, D_MODEL:2 * D_MODEL]) * _dot(yb, wpb_ref[...]))
        x1_ref[...] = x_ref[...] + _dot(merged.astype(BF16), wo_ref[...])


def _sample_mixer(x, ckt, cvt, sinks, inv_row, g_mix, w_in, ln_g, ln_b, ws0, bs0, w_pa, w_pb, w_o):
    n = x.shape[0]
    chunk = SAMPLE_CHUNK
    full = lambda w: pl.BlockSpec((n, w), lambda i: (0, 0))
    cache = pl.BlockSpec((chunk, KV_WIDTH, CHUNK), lambda i: (i, 0, 0))
    return pl.pallas_call(
        functools.partial(_sample_mixer_kernel, chunk=chunk),
        grid=(n // chunk,),
        in_specs=[
            pl.BlockSpec(memory_space=pltpu.SMEM),
            full(D_MODEL), cache, cache,
            _const_spec((1, LANES)),
            _const_spec((1, D_MODEL)),
            _const_spec((D_MODEL, IN_WIDTH)),
            _const_spec((1, GM_WIDTH)),
            _const_spec((1, GM_WIDTH)),
            _const_spec((1, GM_WIDTH)),
            _const_spec((1, GM_WIDTH)),
            _const_spec((GM_WIDTH, D_MODEL)),
            _const_spec((Q_WIDTH, D_MODEL)),
            _const_spec((D_MODEL, D_MODEL)),
        ],
        out_specs=[full(D_MODEL), full(GM_WIDTH), cache, cache],
        out_shape=[
            jax.ShapeDtypeStruct((n, D_MODEL), F32),
            jax.ShapeDtypeStruct((n, GM_WIDTH), F32),
            jax.ShapeDtypeStruct((n, KV_WIDTH, CHUNK), F32),
            jax.ShapeDtypeStruct((n, KV_WIDTH, CHUNK), F32),
        ],
        scratch_shapes=[
            pltpu.VMEM((n, D_MODEL), BF16),
            pltpu.VMEM((n, GM_WIDTH), BF16),
            pltpu.VMEM((N_HEADS * n, LANES), F32),
            pltpu.VMEM((n, KV_WIDTH), F32),
            pltpu.VMEM((n, KV_WIDTH), F32),
            pltpu.VMEM((KV_WIDTH, n), F32),
            pltpu.VMEM((KV_WIDTH, n), F32),
            pltpu.VMEM((N_HEADS * n, LANES), F32),
        ],
        compiler_params=pltpu.CompilerParams(
            dimension_semantics=("arbitrary",), vmem_limit_bytes=VMEM_LIMIT),
        name="sample_mixer",
    )(sinks, x, ckt, cvt, inv_row, g_mix, w_in, ln_g, ln_b, ws0, bs0, w_pa, w_pb, w_o)


def _sample_ffn_kernel(x_ref, g_ref, wua_ref, wub_ref, cwa_ref, cwb_ref, cba_ref, cbb_ref,
                       p0a_ref, p0b_ref, p1a_ref, p1b_ref, wdn_ref, gf_ref,
                       y_ref, za_ref, zb_ref, wua_bf_ref, wub_bf_ref, wdn_bf_ref, acc_ref):
    j = pl.program_id(0)
    x = x_ref[...]
    h = _rms(x, g_ref[...]).astype(BF16)
    wua = wua_ref[...].astype(BF16)
    wub = wub_ref[...].astype(BF16)
    wdn = wdn_ref[...].astype(BF16)
    wua_bf_ref[...] = wua
    wub_bf_ref[...] = wub
    wdn_bf_ref[...] = wdn
    za = _dot(h, wua)
    zb = _dot(h, wub)
    za_ref[...] = za
    zb_ref[...] = zb
    ca = cba_ref[...] + cwa_ref[0:1, :] * p0a_ref[...] + cwa_ref[1:2, :] * p1a_ref[...] + cwa_ref[2:3, :] * za
    cb = cbb_ref[...] + cwb_ref[0:1, :] * p0b_ref[...] + cwb_ref[1:2, :] * p1b_ref[...] + cwb_ref[2:3, :] * zb
    part = _dot((_gelu(ca) * cb).astype(BF16), wdn)

    @pl.when(j == 0)
    def _first():
        acc_ref[...] = part

    @pl.when(j > 0)
    def _rest():
        acc_ref[...] += part

    @pl.when(j == pl.num_programs(0) - 1)
    def _finish():
        y_ref[...] = _rms(x + acc_ref[...], gf_ref[...])


def _sample_ffn(x1, state2d, g_ffn, w_up, conv_w, conv_b, w_down, g_final):
    n = x1.shape[0]
    nchunk = D_FF // FF_CHUNK
    full = lambda w: pl.BlockSpec((n, w), lambda j: (0, 0))
    col = lambda rows, off: pl.BlockSpec((rows, FF_CHUNK), lambda j, off=off: (0, j + off))
    return pl.pallas_call(
        _sample_ffn_kernel,
        grid=(nchunk,),
        in_specs=[
            full(D_MODEL),
            pl.BlockSpec((1, D_MODEL), lambda j: (0, 0)),
            col(D_MODEL, 0), col(D_MODEL, nchunk),
            col(CONV_W, 0), col(CONV_W, nchunk),
            col(1, 0), col(1, nchunk),
            col(n, 0), col(n, nchunk), col(n, 2 * nchunk), col(n, 3 * nchunk),
            pl.BlockSpec((FF_CHUNK, D_MODEL), lambda j: (j, 0)),
            pl.BlockSpec((1, D_MODEL), lambda j: (0, 0)),
        ],
        out_specs=[full(D_MODEL), col(n, 0), col(n, 0), col(D_MODEL, 0), col(D_MODEL, 0),
                   pl.BlockSpec((FF_CHUNK, D_MODEL), lambda j: (j, 0))],
        out_shape=[
            jax.ShapeDtypeStruct((n, D_MODEL), F32),
            jax.ShapeDtypeStruct((n, D_FF), F32),
            jax.ShapeDtypeStruct((n, D_FF), F32),
            jax.ShapeDtypeStruct((D_MODEL, D_FF), BF16),
            jax.ShapeDtypeStruct((D_MODEL, D_FF), BF16),
            jax.ShapeDtypeStruct((D_FF, D_MODEL), BF16),
        ],
        scratch_shapes=[pltpu.VMEM((n, D_MODEL), F32)],
        compiler_params=pltpu.CompilerParams(
            dimension_semantics=("arbitrary",), vmem_limit_bytes=VMEM_LIMIT),
        name="sample_ffn",
    )(x1, g_ffn, w_up, w_up, conv_w, conv_w, conv_b, conv_b,
      state2d, state2d, state2d, state2d, w_down, g_final)


def kernel(x_prompt, x_sample, cache_swa_k, cache_swa_v, state_ffn_conv, g_mix, w_in, ln_v_g, ln_v_b,
           w_s, b_s, sinks, w_pa, w_pb, w_o, g_ffn, w_up, conv_w, conv_b, w_down, g_final):
    depth = g_mix.shape[0]
    assert depth == 1
    nb = x_prompt.shape[0]
    nd = x_sample.shape[0]
    half = HEAD_DIM // 2
    inv = ROPE_THETA ** (-jnp.arange(half, dtype=F32) / half)
    inv_row = jnp.tile(inv, LANES // half)[None, :]

    l = 0
    row = lambda v: v[l][None, :]
    w_in_b = w_in[l].astype(BF16)
    w_pa_b = w_pa[l].astype(BF16)
    w_pb_b = w_pb[l].astype(BF16)
    w_o_b = w_o[l].astype(BF16)
    bs_full = jnp.repeat(b_s[l].T, HEAD_DIM, axis=1)
    ws0 = jnp.repeat(w_s[l][:, 0, 0], HEAD_DIM)[None, :]
    bs0 = bs_full[0:1]

    xs = x_sample.reshape(nd, D_MODEL)
    to_feature_major = lambda c: jnp.transpose(c, (0, 2, 3, 1)).reshape(c.shape[0], -1, CHUNK)
    from_feature_major = lambda c, heads: jnp.transpose(
        c.reshape(c.shape[0], heads, c.shape[1] // heads, CHUNK), (0, 3, 1, 2))[None]
    x1s, gms, cktn, cvtn = _sample_mixer(xs, to_feature_major(cache_swa_k[l]), to_feature_major(cache_swa_v[l]),
                                         sinks[l], inv_row, row(g_mix), w_in_b, row(ln_v_g),
                                         row(ln_v_b), ws0, bs0, w_pa_b, w_pb_b, w_o_b)
    state = state_ffn_conv[l]
    ys, za, zb, w_up_a_b, w_up_b_b, w_dn_b = _sample_ffn(
        x1s, state.reshape(nd, (CONV_W - 1) * 2 * D_FF), row(g_ffn), w_up[l], conv_w[l], row(conv_b),
        w_down[l], g_final[None, :])
    conv_s = jnp.stack([state[:, 1, :], jnp.concatenate([za, zb], axis=1)], axis=1)

    x1p, kp, vp, gmp = _prompt_mixer(x_prompt, sinks[l], inv_row, row(g_mix), w_in_b, row(ln_v_g), row(ln_v_b),
                                     w_s[l], bs_full, w_pa_b, w_pb_b, w_o_b)
    yp, convp = _prompt_ffn(x1p, row(g_ffn), w_up_a_b, w_up_b_b, conv_w[l], row(conv_b), w_dn_b,
                            g_final[None, :])

    return (yp,
            ys.reshape(nd, 1, D_MODEL),
            from_feature_major(kp, KV_HEADS),
            from_feature_major(vp, KV_HEADS),
            from_feature_major(gmp, GM_HEADS),
            convp[None],
            from_feature_major(cktn, KV_HEADS),
            from_feature_major(cvtn, KV_HEADS),
            gms.reshape(1, nd, 1, GM_HEADS, GM_WIDTH // GM_HEADS),
            conv_s[None])
```
